```python
import jax, jax.numpy as jnp
from jax import lax
import numpy as np

D_MODEL = 1024
BATCH = 16
SEQ = 4096
DEPTH = 1

HEAD_DIM = 64
BRANCH_WIDTH = D_MODEL // 2
A_GROUPS = ((128, 1), (512, 4), (2048, 16))
A_HEADS = BRANCH_WIDTH // HEAD_DIM
B_HEADS = BRANCH_WIDTH // HEAD_DIM
B_KV_HEADS = B_HEADS // 4
M_HEADS = 4
M_HEAD_DIM = BRANCH_WIDTH // M_HEADS
N_MEM = 256
GRID_W = 64
ROPE_THETA = 10000.0
Q_BLOCK = 128
NORM_EPS = 1e-6
NEG_INF = -1e30
N_BRANCHES = 3

A_QKV = 3 * len(A_GROUPS) * A_HEADS * HEAD_DIM
B_Q = B_HEADS * HEAD_DIM
B_KV = B_KV_HEADS * HEAD_DIM
M_Q = M_HEADS * M_HEAD_DIM
SPLIT_SIZES = (A_QKV, B_Q, B_KV, B_KV, M_Q, BRANCH_WIDTH, BRANCH_WIDTH, BRANCH_WIDTH, N_BRANCHES * D_MODEL)
IN_WIDTH = A_QKV + B_Q + 2 * B_KV + M_Q + 3 * BRANCH_WIDTH + N_BRANCHES * D_MODEL

kernel_name = 'hybrid_dilated_axial_memory_block'


def rms_norm(x, g):
    xf = x.astype(jnp.float32)
    y = xf * lax.rsqrt(jnp.mean(xf * xf, axis=-1, keepdims=True) + NORM_EPS)
    return (y * g.astype(jnp.float32)).astype(x.dtype)


def rope(x, pos):
    dr = x.shape[-1]
    half = dr // 2
    inv = jnp.power(ROPE_THETA, -jnp.arange(half, dtype=jnp.float32) * 2.0 / dr)
    ang = pos.astype(jnp.float32)[:, None] * inv[None, :]
    cos = jnp.cos(ang)[:, None, :]
    sin = jnp.sin(ang)[:, None, :]
    xf = x.astype(jnp.float32)
    x1, x2 = xf[..., :half], xf[..., half:]
    return jnp.concatenate([x1 * cos - x2 * sin, x1 * sin + x2 * cos], axis=-1).astype(x.dtype)


def axial_rope(x, row_pos, col_pos):
    half = x.shape[-1] // 2
    return jnp.concatenate([rope(x[..., :half], row_pos), rope(x[..., half:], col_pos)], axis=-1)


def dilated_window_attention(q, k, v, window, dilation):
    b, s, h, dh = q.shape
    half = window // (2 * dilation)
    blk = half
    L = s // dilation
    nb = -(-L // blk)
    lp = nb * blk
    z = b * dilation

    def regroup(t):
        return t.reshape(b, L, dilation, h, dh).transpose(0, 2, 1, 3, 4).reshape(z, L, h, dh)

    qg, kg, vg = regroup(q), regroup(k), regroup(v)
    qb = jnp.pad(qg, ((0, 0), (0, lp - L), (0, 0), (0, 0))).reshape(z, nb, blk, h, dh)

    def band(t):
        tp = jnp.pad(t, ((0, 0), (blk, lp - L + blk), (0, 0), (0, 0))).reshape(z, nb + 2, blk, h, dh)
        return jnp.concatenate([tp[:, :-2], tp[:, 1:-1], tp[:, 2:]], axis=2)

    kw, vw = band(kg), band(vg)
    qpos = jnp.arange(nb)[:, None] * blk + jnp.arange(blk)[None, :]
    kpos = (jnp.arange(nb)[:, None] - 1) * blk + jnp.arange(3 * blk)[None, :]
    valid = ((jnp.abs(qpos[:, :, None] - kpos[:, None, :]) <= half)
             & (kpos[:, None, :] >= 0) & (kpos[:, None, :] < L))
    sc = jnp.einsum('znqhd,znkhd->znhqk', qb, kw).astype(jnp.float32) * (dh ** -0.5)
    sc = jnp.where(valid[None, :, None], sc, NEG_INF)
    m = jnp.max(sc, axis=-1, keepdims=True)
    p = jnp.exp(sc - m)
    l = jnp.sum(p, axis=-1, keepdims=True)
    o = jnp.einsum('znhqk,znkhd->znqhd', (p / l).astype(v.dtype), vw)
    lse = (m + jnp.log(l))[..., 0]
    o = o.reshape(b, dilation, lp, h, dh)[:, :, :L].transpose(0, 2, 1, 3, 4).reshape(b, s, h, dh)
    lse = lse.transpose(0, 1, 3, 2).reshape(b, dilation, lp, h)[:, :, :L]
    lse = lse.transpose(0, 2, 1, 3).reshape(b, s, h)
    return o, lse


def setup_inputs(seed: int = 0) -> dict:
    key = jax.random.key(seed)
    ks = jax.random.split(key, 16)
    nrm = jax.random.normal
    f32 = jnp.float32
    return {
        'x': nrm(ks[0], (BATCH, SEQ, D_MODEL), f32),
        'mem': nrm(ks[1], (BATCH, N_MEM, D_MODEL), f32),
        'g_pre': 1.0 + 0.05 * nrm(ks[2], (DEPTH, D_MODEL), f32),
        'w_in': nrm(ks[3], (DEPTH, D_MODEL, IN_WIDTH), f32) * D_MODEL ** -0.5,
        'b_merge': 0.1 * nrm(ks[4], (DEPTH, N_BRANCHES * D_MODEL), f32),
        'q_norm': 1.0 + 0.05 * nrm(ks[5], (DEPTH, HEAD_DIM), f32),
        'k_norm': 1.0 + 0.05 * nrm(ks[6], (DEPTH, HEAD_DIM), f32),
        'g_mem': 1.0 + 0.05 * nrm(ks[7], (DEPTH, D_MODEL), f32),
        'w_mem_kv': nrm(ks[8], (DEPTH, D_MODEL, 2 * M_Q), f32) * D_MODEL ** -0.5,
        'w_br_a': nrm(ks[9], (DEPTH, BRANCH_WIDTH, D_MODEL), f32) * BRANCH_WIDTH ** -0.5,
        'w_br_b': nrm(ks[10], (DEPTH, BRANCH_WIDTH, D_MODEL), f32) * BRANCH_WIDTH ** -0.5,
        'w_br_m': nrm(ks[11], (DEPTH, BRANCH_WIDTH, D_MODEL), f32) * BRANCH_WIDTH ** -0.5,
        'w_out': nrm(ks[12], (DEPTH, D_MODEL, D_MODEL), f32) * D_MODEL ** -0.5,
        'g_post': 1.0 + 0.05 * nrm(ks[13], (DEPTH, D_MODEL), f32),
    }


def reference(x, mem, g_pre, w_in, b_merge, q_norm, k_norm, g_mem, w_mem_kv, w_br_a, w_br_b, w_br_m, w_out, g_post):
    b, s, d = x.shape
    dt = x.dtype
    pos = jnp.arange(s, dtype=jnp.float32)
    rows = s // GRID_W
    row_pos = jnp.repeat(jnp.arange(rows, dtype=jnp.float32), GRID_W)
    col_pos = (jnp.arange(s) % GRID_W).astype(jnp.float32)
    split_at = np.cumsum(SPLIT_SIZES)[:-1].tolist()
    n_groups = len(A_GROUPS)
    kv_rep = B_HEADS // B_KV_HEADS
    n_q_blocks = s // Q_BLOCK

    for layer in range(DEPTH):
        h = rms_norm(x, g_pre[layer])
        a_qkv, bq, bk, bv, mq, ga, gb, gm, mg = jnp.split(h @ w_in[layer], split_at, axis=-1)

        a_qkv = a_qkv.reshape(b, s, 3, n_groups * A_HEADS, HEAD_DIM)
        aq = rope(a_qkv[:, :, 0], pos)
        ak = rope(a_qkv[:, :, 1], pos)
        av = a_qkv[:, :, 2]
        outs, lses = [], []
        for gi, (window, dil) in enumerate(A_GROUPS):
            hs = slice(gi * A_HEADS, (gi + 1) * A_HEADS)
            o_g, lse_g = dilated_window_attention(aq[:, :, hs], ak[:, :, hs], av[:, :, hs], window, dil)
            outs.append(o_g)
            lses.append(lse_g)
        wts = jax.nn.softmax(jnp.stack(lses, axis=0), axis=0).astype(dt)
        oa = jnp.einsum('gbsh,gbshd->bshd', wts, jnp.stack(outs, axis=0)).reshape(b, s, BRANCH_WIDTH)

        qb_ = axial_rope(rms_norm(bq.reshape(b, s, B_HEADS, HEAD_DIM), q_norm[layer]), row_pos, col_pos)
        kb_ = axial_rope(rms_norm(bk.reshape(b, s, B_KV_HEADS, HEAD_DIM), k_norm[layer]), row_pos, col_pos)
        vb_ = bv.reshape(b, s, B_KV_HEADS, HEAD_DIM)
        q_blocks = qb_.reshape(b, n_q_blocks, Q_BLOCK, B_KV_HEADS, kv_rep, HEAD_DIM).transpose(1, 0, 2, 3, 4, 5)

        def attend(qc, kb_=kb_, vb_=vb_):
            sc = jnp.einsum('bqkgd,bskd->bkgqs', qc, kb_).astype(jnp.float32) * (HEAD_DIM ** -0.5)
            p = jax.nn.softmax(sc, axis=-1).astype(vb_.dtype)
            return jnp.einsum('bkgqs,bskd->bqkgd', p, vb_)

        ob = lax.map(attend, q_blocks).transpose(1, 0, 2, 3, 4, 5).reshape(b, s, B_Q)

        kvm = rms_norm(mem, g_mem[layer]) @ w_mem_kv[layer]
        n_mem = mem.shape[1]
        km = kvm[..., :M_Q].reshape(b, n_mem, M_HEADS, M_HEAD_DIM)
        vm = kvm[..., M_Q:].reshape(b, n_mem, M_HEADS, M_HEAD_DIM)
        sm = jnp.einsum('bshd,bnhd->bhsn', mq.reshape(b, s, M_HEADS, M_HEAD_DIM), km).astype(jnp.float32)
        pm = jax.nn.softmax(sm * (M_HEAD_DIM ** -0.5), axis=-1).astype(dt)
        om = jnp.einsum('bhsn,bnhd->bshd', pm, vm).reshape(b, s, M_Q)

        ya = (oa * jax.nn.silu(ga)) @ w_br_a[layer]
        yb = (ob * jax.nn.silu(gb)) @ w_br_b[layer]
        ym = (om * jax.nn.silu(gm)) @ w_br_m[layer]
        gates = jax.nn.sigmoid((mg + b_merge[layer]).astype(jnp.float32)).astype(dt).reshape(b, s, N_BRANCHES, d)
        merged = gates[:, :, 0] * ya + gates[:, :, 1] * yb + gates[:, :, 2] * ym
        out = merged @ w_out[layer]
        x = x + rms_norm(out, g_post[layer])
    return x
```

```python
import functools

import jax
import jax.numpy as jnp
import numpy as np
from jax import lax
from jax.experimental import pallas as pl
from jax.experimental.pallas import tpu as pltpu

D_MODEL = 1024
HEAD_DIM = 64
BRANCH_WIDTH = 512
A_GROUPS = ((128, 1), (512, 4), (2048, 16))
A_HEADS = 8
B_HEADS = 8
B_KV_HEADS = 2
M_HEADS = 4
M_HEAD_DIM = 128
GRID_W = 64
ROPE_THETA = 10000.0
NORM_EPS = 1e-6
NEG_INF = -1e30
N_BRANCHES = 3

A_QKV = 3 * len(A_GROUPS) * A_HEADS * HEAD_DIM
B_Q = B_HEADS * HEAD_DIM
B_KV = B_KV_HEADS * HEAD_DIM
B_QKV = B_Q + 2 * B_KV
M_Q = M_HEADS * M_HEAD_DIM
GATE_W = 3 * BRANCH_WIDTH + N_BRANCHES * D_MODEL

LANES = 128
A_QBLK = 128
A_KWIN = 256
A_HALF = 64
LSE_LANES_PER_HEAD = LANES // A_HEADS
VMEM_LIMIT = 56 * 1024 * 1024

F32 = jnp.float32
BF16 = jnp.bfloat16


def _cparams(sem):
    return pltpu.CompilerParams(dimension_semantics=sem, vmem_limit_bytes=VMEM_LIMIT)


def _rms_norm_rows(xf, g):
    return xf * lax.rsqrt(jnp.mean(xf * xf, axis=-1, keepdims=True) + NORM_EPS) * g


def _rotate(x, c, s_fwd, s_bwd, shift):
    return (x * c + pltpu.roll(x, shift, axis=1) * s_fwd
            + pltpu.roll(x, LANES - shift, axis=1) * s_bwd)


def _proj_a_kernel(x_ref, g_ref, w_ref, t_ref, o_ref, h_ref):
    j = pl.program_id(1)

    @pl.when(j == 0)
    def _():
        h_ref[...] = _rms_norm_rows(x_ref[...], g_ref[...]).astype(BF16)

    acc = jnp.dot(h_ref[...], w_ref[...], preferred_element_type=F32)

    @pl.when(j % 3 < 2)
    def _():
        c, sf, sb = t_ref[0], t_ref[1], t_ref[2]
        for ch in range(BRANCH_WIDTH // LANES):
            sl = slice(ch * LANES, (ch + 1) * LANES)
            o_ref[:, sl] = _rotate(acc[:, sl], c, sf, sb, HEAD_DIM // 2).astype(BF16)

    @pl.when(j % 3 == 2)
    def _():
        o_ref[...] = acc.astype(BF16)


def _proj_a(x2, g_pre, w_a, t_a, s, tm):
    m = x2.shape[0]
    n_sblk = s // tm
    return pl.pallas_call(
        _proj_a_kernel,
        grid=(m // tm, A_QKV // BRANCH_WIDTH),
        in_specs=[
            pl.BlockSpec((tm, D_MODEL), lambda i, j: (i, 0)),
            pl.BlockSpec((1, D_MODEL), lambda i, j: (0, 0)),
            pl.BlockSpec((D_MODEL, BRANCH_WIDTH), lambda i, j: (0, j)),
            pl.BlockSpec((3, tm, LANES), lambda i, j: (0, i % n_sblk, 0)),
        ],
        out_specs=pl.BlockSpec((tm, BRANCH_WIDTH), lambda i, j: (i, j)),
        out_shape=jax.ShapeDtypeStruct((m, A_QKV), BF16),
        scratch_shapes=[pltpu.VMEM((tm, D_MODEL), BF16)],
        compiler_params=_cparams(("parallel", "arbitrary")),
        name="proj_a",
    )(x2, g_pre, w_a, t_a)


def _proj_bm_kernel(x_ref, g_ref, w_ref, t_ref, gq_ref, gk_ref, ones_ref, ob_ref, om_ref):
    h = _rms_norm_rows(x_ref[...], g_ref[...]).astype(BF16)
    acc = jnp.dot(h, w_ref[...], preferred_element_type=F32)
    c, sf, sb = t_ref[0], t_ref[1], t_ref[2]
    ones_bd = ones_ref[...]
    n_qk = (B_Q + B_KV) // LANES
    for ch in range(n_qk):
        sl = slice(ch * LANES, (ch + 1) * LANES)
        y = acc[:, sl]
        sq = y * y
        hi = sq.astype(BF16)
        lo = (sq - hi.astype(F32)).astype(BF16)
        ss = (jnp.dot(hi, ones_bd, preferred_element_type=F32)
              + jnp.dot(lo, ones_bd, preferred_element_type=F32))
        gain = gq_ref[...] if ch < B_Q // LANES else gk_ref[...]
        y = y * lax.rsqrt(ss * (1.0 / HEAD_DIM) + NORM_EPS) * gain
        ob_ref[:, sl] = _rotate(y, c, sf, sb, HEAD_DIM // 4).astype(BF16)
    ob_ref[:, B_Q + B_KV:] = acc[:, B_Q + B_KV:B_QKV].astype(BF16)
    om_ref[...] = acc[:, B_QKV:].astype(BF16)


def _proj_bm(x2, g_pre, w_bm, t_b, gq, gk, ones_bd, s, tm):
    m = x2.shape[0]
    n_sblk = s // tm
    n = B_QKV + M_Q
    return pl.pallas_call(
        _proj_bm_kernel,
        grid=(m // tm,),
        in_specs=[
            pl.BlockSpec((tm, D_MODEL), lambda i: (i, 0)),
            pl.BlockSpec((1, D_MODEL), lambda i: (0, 0)),
            pl.BlockSpec((D_MODEL, n), lambda i: (0, 0)),
            pl.BlockSpec((3, tm, LANES), lambda i: (0, i % n_sblk, 0)),
            pl.BlockSpec((1, LANES), lambda i: (0, 0)),
            pl.BlockSpec((1, LANES), lambda i: (0, 0)),
            pl.BlockSpec((LANES, LANES), lambda i: (0, 0)),
        ],
        out_specs=[
            pl.BlockSpec((tm, B_QKV), lambda i: (i, 0)),
            pl.BlockSpec((tm, M_Q), lambda i: (i, 0)),
        ],
        out_shape=[jax.ShapeDtypeStruct((m, B_QKV), BF16),
                   jax.ShapeDtypeStruct((m, M_Q), BF16)],
        compiler_params=_cparams(("parallel",)),
        name="proj_bm",
    )(x2, g_pre, w_bm, t_b, gq, gk, ones_bd)


def _mem_kv_kernel(x_ref, g_ref, w_ref, o_ref):
    h = _rms_norm_rows(x_ref[...], g_ref[...]).astype(BF16)
    o_ref[...] = jnp.dot(h, w_ref[...], preferred_element_type=F32).astype(BF16)


def _mem_kv(mem2, g_mem, w_kv, tm):
    m = mem2.shape[0]
    n = w_kv.shape[1]
    return pl.pallas_call(
        _mem_kv_kernel,
        grid=(m // tm,),
        in_specs=[
            pl.BlockSpec((tm, D_MODEL), lambda i: (i, 0)),
            pl.BlockSpec((1, D_MODEL), lambda i: (0, 0)),
            pl.BlockSpec((D_MODEL, n), lambda i: (0, 0)),
        ],
        out_specs=pl.BlockSpec((tm, n), lambda i: (i, 0)),
        out_shape=jax.ShapeDtypeStruct((m, n), BF16),
        compiler_params=_cparams(("parallel",)),
        name="mem_kv",
    )(mem2, g_mem, w_kv)


def _attn_a_kernel(q_ref, k_ref, v_ref, o_ref, lse_ref, *, tq, cls_len):
    t = pl.program_id(2)
    diff = (lax.broadcasted_iota(jnp.int32, (A_QBLK, A_KWIN), 0)
            - lax.broadcasted_iota(jnp.int32, (A_QBLK, A_KWIN), 1))
    head_of_lane = lax.broadcasted_iota(jnp.int32, (A_QBLK, LANES), 1) // LSE_LANES_PER_HEAD

    def body(qi, carry):
        q0 = t * tq + qi * A_QBLK
        ks = pl.multiple_of(jnp.clip(q0 - A_HALF, 0, cls_len - A_KWIN), A_HALF)
        valid = jnp.abs(diff + (q0 - ks)) <= A_HALF
        qoff = pl.multiple_of(qi * A_QBLK, A_QBLK)
        qblk = q_ref[pl.ds(qoff, A_QBLK), :]
        kblk = k_ref[pl.ds(ks, A_KWIN), :]
        vblk = v_ref[pl.ds(ks, A_KWIN), :]
        lse_tile = jnp.zeros((A_QBLK, LANES), F32)
        outs = []
        for h in range(A_HEADS):
            hs = slice(h * HEAD_DIM, (h + 1) * HEAD_DIM)
            sc = lax.dot_general(qblk[:, hs], kblk[:, hs], (((1,), (1,)), ((), ())),
                                 preferred_element_type=F32)
            sc = jnp.where(valid, sc, NEG_INF)
            mx = jnp.max(sc, axis=-1, keepdims=True)
            p = jnp.exp(sc - mx)
            l = jnp.sum(p, axis=-1, keepdims=True)
            o = jnp.dot(p.astype(BF16), vblk[:, hs], preferred_element_type=F32)
            outs.append(o / l)
            lse_tile = jnp.where(head_of_lane == h, mx + jnp.log(l), lse_tile)
        o_ref[pl.ds(qoff, A_QBLK), :] = jnp.concatenate(outs, axis=1).astype(BF16)
        lse_ref[pl.ds(qoff, A_QBLK), :] = lse_tile
        return carry

    lax.fori_loop(0, tq // A_QBLK, body, 0)


def _attn_a(qkv_a, b, s, gi, dil):
    cls_len = s // dil
    assert cls_len >= A_KWIN and cls_len % A_QBLK == 0
    tq = min(cls_len, 1024)
    n_col = A_QKV // BRANCH_WIDTH
    view = qkv_a.reshape(b, cls_len, dil * A_QKV)
    kern = functools.partial(_attn_a_kernel, tq=tq, cls_len=cls_len)
    o, lse = pl.pallas_call(
        kern,
        grid=(b, dil, cls_len // tq),
        in_specs=[
            pl.BlockSpec((None, tq, BRANCH_WIDTH), lambda bi, r, t: (bi, t, r * n_col + 3 * gi)),
            pl.BlockSpec((None, cls_len, BRANCH_WIDTH), lambda bi, r, t: (bi, 0, r * n_col + 3 * gi + 1)),
            pl.BlockSpec((None, cls_len, BRANCH_WIDTH), lambda bi, r, t: (bi, 0, r * n_col + 3 * gi + 2)),
        ],
        out_specs=[
            pl.BlockSpec((None, tq, BRANCH_WIDTH), lambda bi, r, t: (bi, t, r)),
            pl.BlockSpec((None, tq, LANES), lambda bi, r, t: (bi, t, r)),
        ],
        out_shape=[jax.ShapeDtypeStruct((b, cls_len, dil * BRANCH_WIDTH), BF16),
                   jax.ShapeDtypeStruct((b, cls_len, dil * LANES), F32)],
        compiler_params=_cparams(("parallel", "parallel", "arbitrary")),
        name=f"attn_a{gi}",
    )(view, view, view)
    return o.reshape(b * s, BRANCH_WIDTH), lse.reshape(b * s, LANES)


def _attn_b_kernel(q_ref, k_ref, v_ref, o_ref):
    rep = B_HEADS // B_KV_HEADS
    outs = []
    for kh in range(B_KV_HEADS):
        ks = slice(kh * HEAD_DIM, (kh + 1) * HEAD_DIM)
        k = k_ref[:, ks]
        v = v_ref[:, ks]
        for r in range(rep):
            h = kh * rep + r
            q = q_ref[:, h * HEAD_DIM:(h + 1) * HEAD_DIM]
            sc = lax.dot_general(q, k, (((1,), (1,)), ((), ())), preferred_element_type=F32)
            mx = jnp.max(sc, axis=-1, keepdims=True)
            p = jnp.exp(sc - mx)
            l = jnp.sum(p, axis=-1, keepdims=True)
            o = jnp.dot(p.astype(BF16), v, preferred_element_type=F32)
            outs.append(o / l)
    o_ref[...] = jnp.concatenate(outs, axis=1).astype(BF16)


def _attn_b(qkv_b, b, s, tq):
    view = qkv_b.reshape(b, s, B_QKV)
    o = pl.pallas_call(
        _attn_b_kernel,
        grid=(b, s // tq),
        in_specs=[
            pl.BlockSpec((None, tq, B_Q), lambda bi, t: (bi, t, 0)),
            pl.BlockSpec((None, s, B_KV), lambda bi, t: (bi, 0, B_Q // B_KV)),
            pl.BlockSpec((None, s, B_KV), lambda bi, t: (bi, 0, B_Q // B_KV + 1)),
        ],
        out_specs=pl.BlockSpec((None, tq, B_Q), lambda bi, t: (bi, t, 0)),
        out_shape=jax.ShapeDtypeStruct((b, s, B_Q), BF16),
        compiler_params=_cparams(("parallel", "arbitrary")),
        name="attn_b",
    )(view, view, view)
    return o.reshape(b * s, B_Q)


def _attn_m_kernel(q_ref, k_ref, v_ref, o_ref):
    scale = M_HEAD_DIM ** -0.5
    outs = []
    for h in range(M_HEADS):
        hs = slice(h * M_HEAD_DIM, (h + 1) * M_HEAD_DIM)
        sc = lax.dot_general(q_ref[:, hs], k_ref[:, hs], (((1,), (1,)), ((), ())),
                             preferred_element_type=F32) * scale
        mx = jnp.max(sc, axis=-1, keepdims=True)
        p = jnp.exp(sc - mx)
        l = jnp.sum(p, axis=-1, keepdims=True)
        o = jnp.dot(p.astype(BF16), v_ref[:, hs], preferred_element_type=F32)
        outs.append(o / l)
    o_ref[...] = jnp.concatenate(outs, axis=1).astype(BF16)


def _attn_m(mq, kvm, b, s, n_mem, tq):
    qv = mq.reshape(b, s, M_Q)
    kv = kvm.reshape(b, n_mem, 2 * M_Q)
    o = pl.pallas_call(
        _attn_m_kernel,
        grid=(b, s // tq),
        in_specs=[
            pl.BlockSpec((None, tq, M_Q), lambda bi, t: (bi, t, 0)),
            pl.BlockSpec((None, n_mem, M_Q), lambda bi, t: (bi, 0, 0)),
            pl.BlockSpec((None, n_mem, M_Q), lambda bi, t: (bi, 0, 1)),
        ],
        out_specs=pl.BlockSpec((None, tq, M_Q), lambda bi, t: (bi, t, 0)),
        out_shape=jax.ShapeDtypeStruct((b, s, M_Q), BF16),
        compiler_params=_cparams(("parallel", "arbitrary")),
        name="attn_m",
    )(qv, kv, kv)
    return o.reshape(b * s, M_Q)


def _final_kernel(x_ref, oa0_ref, oa1_ref, oa2_ref, l0_ref, l1_ref, l2_ref, ob_ref, om_ref,
                  gpre_ref, wg_ref, bm_ref, wa_ref, wb_ref, wm_ref, wout_ref, gpost_ref,
                  exp_ref, o_ref):
    xf = x_ref[...]
    h = _rms_norm_rows(xf, gpre_ref[...]).astype(BF16)

    lses = [l0_ref[...], l1_ref[...], l2_ref[...]]
    mx = jnp.maximum(jnp.maximum(lses[0], lses[1]), lses[2])
    es = [jnp.exp(l - mx) for l in lses]
    inv_z = 1.0 / (es[0] + es[1] + es[2])
    expand = exp_ref[...]
    oa = None
    for e, o_ref_g in zip(es, (oa0_ref, oa1_ref, oa2_ref)):
        w = e * inv_z
        hi = w.astype(BF16)
        lo = (w - hi.astype(F32)).astype(BF16)
        w_wide = (jnp.dot(hi, expand, preferred_element_type=F32)
                  + jnp.dot(lo, expand, preferred_element_type=F32))
        term = w_wide * o_ref_g[...].astype(F32)
        oa = term if oa is None else oa + term

    def branch(o, idx, w_ref):
        gs = slice(idx * BRANCH_WIDTH, (idx + 1) * BRANCH_WIDTH)
        ms = slice(3 * BRANCH_WIDTH + idx * D_MODEL, 3 * BRANCH_WIDTH + (idx + 1) * D_MODEL)
        ms_b = slice(idx * D_MODEL, (idx + 1) * D_MODEL)
        gate = jnp.dot(h, wg_ref[:, gs], preferred_element_type=F32)
        gate = gate * jax.nn.sigmoid(gate)
        y = jnp.dot((o * gate).astype(BF16), w_ref[...], preferred_element_type=F32)
        mg = jnp.dot(h, wg_ref[:, ms], preferred_element_type=F32) + bm_ref[:, ms_b]
        return jax.nn.sigmoid(mg) * y

    merged = branch(oa, 0, wa_ref)
    merged = merged + branch(ob_ref[...].astype(F32), 1, wb_ref)
    merged = merged + branch(om_ref[...].astype(F32), 2, wm_ref)
    out = jnp.dot(merged.astype(BF16), wout_ref[...], preferred_element_type=F32)
    o_ref[...] = xf + _rms_norm_rows(out, gpost_ref[...])


def _final(x2, oas, lses, ob, om, g_pre, w_g, b_merge, w_a, w_b, w_m, w_out, g_post, expand, tm):
    m = x2.shape[0]
    row = lambda n: pl.BlockSpec((tm, n), lambda i: (i, 0))
    full = lambda a: pl.BlockSpec(a.shape, lambda i: (0, 0))
    consts = (g_pre, w_g, b_merge, w_a, w_b, w_m, w_out, g_post, expand)
    return pl.pallas_call(
        _final_kernel,
        grid=(m // tm,),
        in_specs=([row(D_MODEL)] + [row(BRANCH_WIDTH)] * 3 + [row(LANES)] * 3
                  + [row(BRANCH_WIDTH)] * 2 + [full(a) for a in consts]),
        out_specs=row(D_MODEL),
        out_shape=jax.ShapeDtypeStruct((m, D_MODEL), F32),
        compiler_params=_cparams(("parallel",)),
        name="final",
    )(x2, *oas, *lses, ob, om, *consts)


def _rotary_tables(pos_per_lane, inv_per_lane, is_second_half):
    ang = pos_per_lane * inv_per_lane[None, :]
    cos, sin = jnp.cos(ang), jnp.sin(ang)
    zero = jnp.zeros_like(sin)
    return jnp.stack([cos,
                      jnp.where(is_second_half[None, :], sin, zero),
                      jnp.where(is_second_half[None, :], zero, -sin)]).astype(F32)


def _tables_a(s):
    half = HEAD_DIM // 2
    lane = np.arange(LANES)
    inv = jnp.power(ROPE_THETA, -jnp.asarray(lane % half, F32) * 2.0 / HEAD_DIM)
    pos = jnp.broadcast_to(jnp.arange(s, dtype=F32)[:, None], (s, LANES))
    return _rotary_tables(pos, inv, jnp.asarray((lane % HEAD_DIM) >= half))


def _tables_b(s):
    dr = HEAD_DIM // 2
    half = dr // 2
    lane = np.arange(LANES)
    inv = jnp.power(ROPE_THETA, -jnp.asarray(lane % half, F32) * 2.0 / dr)
    row_pos = jnp.repeat(jnp.arange(s // GRID_W, dtype=F32), GRID_W)
    col_pos = (jnp.arange(s) % GRID_W).astype(F32)
    uses_col = jnp.asarray((lane % HEAD_DIM) >= dr)
    pos = jnp.where(uses_col[None, :], col_pos[:, None], row_pos[:, None])
    return _rotary_tables(pos, inv, jnp.asarray((lane % dr) >= half))


def kernel(x, mem, g_pre, w_in, b_merge, q_norm, k_norm, g_mem, w_mem_kv, w_br_a, w_br_b, w_br_m, w_out, g_post):
    b, s, d = x.shape
    n_mem = mem.shape[1]
    depth = g_pre.shape[0]
    n_groups = len(A_GROUPS)

    t_a = _tables_a(s)
    t_b = _tables_b(s)
    lane = np.arange(LANES)
    ones_bd = jnp.asarray((lane[:, None] // HEAD_DIM) == (lane[None, :] // HEAD_DIM), BF16)
    wide = np.arange(BRANCH_WIDTH)
    expand = jnp.asarray(lane[:, None] == (wide[None, :] // HEAD_DIM) * LSE_LANES_PER_HEAD, BF16)

    for layer in range(depth):
        w = w_in[layer]
        w_a = w[:, :A_QKV].reshape(d, 3, n_groups, BRANCH_WIDTH)
        w_a = w_a * jnp.asarray([HEAD_DIM ** -0.5, 1.0, 1.0], F32)[None, :, None, None]
        w_a = w_a.transpose(0, 2, 1, 3).reshape(d, A_QKV).astype(BF16)
        off = A_QKV
        w_bm = w[:, off:off + B_QKV + M_Q].astype(BF16)
        off += B_QKV + M_Q
        w_g = w[:, off:].astype(BF16)
        gq = jnp.tile(q_norm[layer] * HEAD_DIM ** -0.5, LANES // HEAD_DIM)[None, :]
        gk = jnp.tile(k_norm[layer], LANES // HEAD_DIM)[None, :]

        x2 = x.reshape(b * s, d)
        gp = g_pre[layer][None, :]
        qkv_a = _proj_a(x2, gp, w_a, t_a, s, tm=1024)
        qkv_b, mq = _proj_bm(x2, gp, w_bm, t_b, gq, gk, ones_bd, s, tm=512)
        kvm = _mem_kv(mem.reshape(b * n_mem, d), g_mem[layer][None, :],
                      w_mem_kv[layer].astype(BF16), tm=n_mem)

        oas, lses = [], []
        for gi, (_, dil) in enumerate(A_GROUPS):
            o_g, lse_g = _attn_a(qkv_a, b, s, gi, dil)
            oas.append(o_g)
            lses.append(lse_g)
        ob = _attn_b(qkv_b, b, s, tq=256)
        om = _attn_m(mq, kvm, b, s, n_mem, tq=1024)

        x2 = _final(x2, oas, lses, ob, om, gp, w_g, b_merge[layer][None, :],
                    w_br_a[layer].astype(BF16), w_br_b[layer].astype(BF16),
                    w_br_m[layer].astype(BF16), w_out[layer].astype(BF16),
                    g_post[layer][None, :], expand, tm=256)
        x = x2.reshape(b, s, d)
    return x
```

```python
import functools

import jax
import jax.numpy as jnp
import numpy as np
from jax import lax
from jax.experimental import pallas as pl
from jax.experimental.pallas import tpu as pltpu

D_MODEL = 1024
HEAD_DIM = 64
BRANCH_WIDTH = 512
A_GROUPS = ((128, 1), (512, 4), (2048, 16))
A_HEADS = 8
B_HEADS = 8
B_KV_HEADS = 2
M_HEADS = 4
M_HEAD_DIM = 128
GRID_W = 64
ROPE_THETA = 10000.0
NORM_EPS = 1e-6
NEG_INF = -1e30
N_BRANCHES = 3

A_QKV = 3 * len(A_GROUPS) * A_HEADS * HEAD_DIM
B_Q = B_HEADS * HEAD_DIM
B_KV = B_KV_HEADS * HEAD_DIM
B_QKV = B_Q + 2 * B_KV
M_Q = M_HEADS * M_HEAD_DIM
GATE_W = 3 * BRANCH_WIDTH + N_BRANCHES * D_MODEL

LANES = 128
A_QBLK = 128
A_KWIN = 256
A_HALF = 64
LSE_LANES_PER_HEAD = LANES // A_HEADS
PERM_TILE = 1024
ROW_CHUNK = 256
VMEM_LIMIT = 56 * 1024 * 1024

F32 = jnp.float32
BF16 = jnp.bfloat16


def _cparams(sem):
    return pltpu.CompilerParams(dimension_semantics=sem, vmem_limit_bytes=VMEM_LIMIT)


def _rms_norm_rows(xf, g):
    return xf * lax.rsqrt(jnp.mean(xf * xf, axis=-1, keepdims=True) + NORM_EPS) * g


def _rotate(x, c, s_fwd, s_bwd, shift):
    return (x * c + pltpu.roll(x, shift, axis=1) * s_fwd
            + pltpu.roll(x, LANES - shift, axis=1) * s_bwd)


def _proj_a_kernel(x_ref, g_ref, w_ref, t_ref, o_ref, hf_ref, hp_ref, *, tm, dils):
    j = pl.program_id(1)
    n_lane = D_MODEL // LANES

    @pl.when(j == 0)
    def _():
        h = _rms_norm_rows(x_ref[...], g_ref[...])
        hb = h.astype(BF16)
        for ch in range(n_lane):
            hf_ref[ch] = h[:, ch * LANES:(ch + 1) * LANES]
        for gi, dil in enumerate(dils):
            if dil == 1:
                hp_ref[gi] = hb
                continue
            c = tm // dil
            for r in range(dil):
                for ch in range(n_lane):
                    hp_ref[gi, r * c:(r + 1) * c, ch * LANES:(ch + 1) * LANES] = (
                        hf_ref[ch, pl.ds(r, c, stride=dil), :].astype(BF16))

    g = j // 3

    def compute(with_rope):
        for rc in range(tm // ROW_CHUNK):
            rows = slice(rc * ROW_CHUNK, (rc + 1) * ROW_CHUNK)
            acc = jnp.dot(hp_ref[g, rows, :], w_ref[...], preferred_element_type=F32)
            if with_rope:
                c, sf, sb = t_ref[0, rows, :], t_ref[1, rows, :], t_ref[2, rows, :]
                for ch in range(BRANCH_WIDTH // LANES):
                    sl = slice(ch * LANES, (ch + 1) * LANES)
                    o_ref[rows, sl] = _rotate(acc[:, sl], c, sf, sb, HEAD_DIM // 2).astype(BF16)
            else:
                o_ref[rows, :] = acc.astype(BF16)

    @pl.when(j % 3 < 2)
    def _():
        compute(True)

    @pl.when(j % 3 == 2)
    def _():
        compute(False)


def _proj_a(x2, g_pre, w_a, t_a, s, dils):
    m = x2.shape[0]
    tm = PERM_TILE
    n_sblk = s // tm
    kern = functools.partial(_proj_a_kernel, tm=tm, dils=dils)
    return pl.pallas_call(
        kern,
        grid=(m // tm, A_QKV // BRANCH_WIDTH),
        in_specs=[
            pl.BlockSpec((tm, D_MODEL), lambda i, j: (i, 0)),
            pl.BlockSpec((1, D_MODEL), lambda i, j: (0, 0)),
            pl.BlockSpec((D_MODEL, BRANCH_WIDTH), lambda i, j: (0, j)),
            pl.BlockSpec((None, 3, tm, LANES), lambda i, j: (j // 3, 0, i % n_sblk, 0)),
        ],
        out_specs=pl.BlockSpec((tm, BRANCH_WIDTH), lambda i, j: (i, j)),
        out_shape=jax.ShapeDtypeStruct((m, A_QKV), BF16),
        scratch_shapes=[pltpu.VMEM((D_MODEL // LANES, tm, LANES), F32),
                        pltpu.VMEM((len(dils), tm, D_MODEL), BF16)],
        compiler_params=_cparams(("parallel", "arbitrary")),
        name="proj_a",
    )(x2, g_pre, w_a, t_a)


def _proj_bm_kernel(x_ref, g_ref, w_ref, t_ref, gq_ref, gk_ref, ones_ref, ob_ref, om_ref):
    h = _rms_norm_rows(x_ref[...], g_ref[...]).astype(BF16)
    acc = jnp.dot(h, w_ref[...], preferred_element_type=F32)
    c, sf, sb = t_ref[0], t_ref[1], t_ref[2]
    ones_bd = ones_ref[...]
    n_qk = (B_Q + B_KV) // LANES
    for ch in range(n_qk):
        sl = slice(ch * LANES, (ch + 1) * LANES)
        y = acc[:, sl]
        sq = y * y
        hi = sq.astype(BF16)
        lo = (sq - hi.astype(F32)).astype(BF16)
        ss = (jnp.dot(hi, ones_bd, preferred_element_type=F32)
              + jnp.dot(lo, ones_bd, preferred_element_type=F32))
        gain = gq_ref[...] if ch < B_Q // LANES else gk_ref[...]
        y = y * lax.rsqrt(ss * (1.0 / HEAD_DIM) + NORM_EPS) * gain
        ob_ref[:, sl] = _rotate(y, c, sf, sb, HEAD_DIM // 4).astype(BF16)
    ob_ref[:, B_Q + B_KV:] = acc[:, B_Q + B_KV:B_QKV].astype(BF16)
    om_ref[...] = acc[:, B_QKV:].astype(BF16)


def _proj_bm(x2, g_pre, w_bm, t_b, gq, gk, ones_bd, s, tm):
    m = x2.shape[0]
    n_sblk = s // tm
    n = B_QKV + M_Q
    return pl.pallas_call(
        _proj_bm_kernel,
        grid=(m // tm,),
        in_specs=[
            pl.BlockSpec((tm, D_MODEL), lambda i: (i, 0)),
            pl.BlockSpec((1, D_MODEL), lambda i: (0, 0)),
            pl.BlockSpec((D_MODEL, n), lambda i: (0, 0)),
            pl.BlockSpec((3, tm, LANES), lambda i: (0, i % n_sblk, 0)),
            pl.BlockSpec((1, LANES), lambda i: (0, 0)),
            pl.BlockSpec((1, LANES), lambda i: (0, 0)),
            pl.BlockSpec((LANES, LANES), lambda i: (0, 0)),
        ],
        out_specs=[
            pl.BlockSpec((tm, B_QKV), lambda i: (i, 0)),
            pl.BlockSpec((tm, M_Q), lambda i: (i, 0)),
        ],
        out_shape=[jax.ShapeDtypeStruct((m, B_QKV), BF16),
                   jax.ShapeDtypeStruct((m, M_Q), BF16)],
        compiler_params=_cparams(("parallel",)),
        name="proj_bm",
    )(x2, g_pre, w_bm, t_b, gq, gk, ones_bd)


def _mem_kv_kernel(x_ref, g_ref, w_ref, o_ref):
    h = _rms_norm_rows(x_ref[...], g_ref[...]).astype(BF16)
    o_ref[...] = jnp.dot(h, w_ref[...], preferred_element_type=F32).astype(BF16)


def _mem_kv(mem2, g_mem, w_kv, tm):
    m = mem2.shape[0]
    n = w_kv.shape[1]
    return pl.pallas_call(
        _mem_kv_kernel,
        grid=(m // tm,),
        in_specs=[
            pl.BlockSpec((tm, D_MODEL), lambda i: (i, 0)),
            pl.BlockSpec((1, D_MODEL), lambda i: (0, 0)),
            pl.BlockSpec((D_MODEL, n), lambda i: (0, 0)),
        ],
        out_specs=pl.BlockSpec((tm, n), lambda i: (i, 0)),
        out_shape=jax.ShapeDtypeStruct((m, n), BF16),
        compiler_params=_cparams(("parallel",)),
        name="mem_kv",
    )(mem2, g_mem, w_kv)


def _band_consts():
    lane = lax.broadcasted_iota(jnp.int32, (A_QBLK, LANES), 1)
    return dict(first_half=lane < HEAD_DIM, head_of_lane=lane // LSE_LANES_PER_HEAD,
                ones=jnp.ones((A_KWIN, LANES), BF16))


def _band_block(cst, q_ref, k_ref, v_ref, bias_ref, eye_ref, o_ref, lse_ref, s_scr, p_scr,
                q_row, k_row, out_row, band_type):
    first_half, head_of_lane, ones = cst["first_half"], cst["head_of_lane"], cst["ones"]
    q_row = pl.multiple_of(q_row, A_QBLK)
    k_row = pl.multiple_of(k_row, A_HALF)
    out_row = pl.multiple_of(out_row, A_QBLK)
    bias_t = bias_ref[band_type]
    eye2 = eye_ref[...]
    n_pair = A_HEADS // 2
    nt = (((1,), (1,)), ((), ()))
    for j in range(n_pair):
        ps = slice(j * LANES, (j + 1) * LANES)
        qp = q_ref[pl.ds(q_row, A_QBLK), ps]
        kp = k_ref[pl.ds(k_row, A_KWIN), ps]
        zero = jnp.zeros_like(qp)
        q2 = jnp.concatenate([jnp.where(first_half, qp, zero), jnp.where(first_half, zero, qp)], axis=0)
        s_scr[j] = lax.dot_general(jnp.concatenate([q2, eye2], axis=1),
                                   jnp.concatenate([kp, bias_t], axis=1), nt,
                                   preferred_element_type=F32)
    mxs = []
    for j in range(n_pair):
        sc = s_scr[j]
        mx = jnp.max(sc, axis=-1, keepdims=True)
        p_scr[j] = jnp.exp(sc - mx).astype(BF16)
        mxs.append(mx)
    m_tile = jnp.zeros((A_QBLK, LANES), F32)
    l_tile = jnp.ones((A_QBLK, LANES), F32)
    for j in range(n_pair):
        ps = slice(j * LANES, (j + 1) * LANES)
        vp = v_ref[pl.ds(k_row, A_KWIN), ps]
        ov = jnp.dot(p_scr[j], jnp.concatenate([vp, ones], axis=1), preferred_element_type=F32)
        o_sel = jnp.where(first_half, ov[:A_QBLK, :LANES], ov[A_QBLK:, :LANES])
        l_sel = jnp.where(first_half, ov[:A_QBLK, LANES:], ov[A_QBLK:, LANES:])
        o_ref[pl.ds(out_row, A_QBLK), ps] = (o_sel / l_sel).astype(BF16)
        for e in range(2):
            rs = slice(e * A_QBLK, (e + 1) * A_QBLK)
            m_tile = jnp.where(head_of_lane == 2 * j + e, mxs[j][rs], m_tile)
            l_tile = jnp.where(head_of_lane == 2 * j + e, ov[rs, LANES:], l_tile)
    lse_ref[pl.ds(out_row, A_QBLK), :] = m_tile + jnp.log(l_tile)


def _window_start(q0, cls_len):
    return jnp.clip(q0 - A_HALF, 0, cls_len - A_KWIN)


def _attn_a_natural_kernel(q_ref, k_ref, v_ref, bias_ref, eye_ref, o_ref, lse_ref, s_scr, p_scr,
                           *, tq, cls_len):
    t = pl.program_id(1)
    cst = _band_consts()

    def body(qi, carry):
        q0 = t * tq + qi * A_QBLK
        k0 = _window_start(q0, cls_len)
        _band_block(cst, q_ref, k_ref, v_ref, bias_ref, eye_ref, o_ref, lse_ref, s_scr, p_scr,
                    qi * A_QBLK, k0, qi * A_QBLK, (q0 - k0) // A_HALF)
        return carry

    lax.fori_loop(0, tq // A_QBLK, body, 0, unroll=2)


def _attn_a_classes_kernel(q_ref, k_ref, v_ref, bias_ref, eye_ref, o_ref, lse_ref,
                           qs, ks, vs, os_, ls_, s_scr, p_scr, *, n_tiles, n_cls, c, cls_len):
    for t in range(n_tiles):
        for cl in range(n_cls):
            rows = slice(cl * cls_len + t * c, cl * cls_len + (t + 1) * c)
            qs[rows, :] = q_ref[t, cl]
            ks[rows, :] = k_ref[t, cl]
            vs[rows, :] = v_ref[t, cl]
    cst = _band_consts()
    nq = cls_len // A_QBLK

    def body(blk, carry):
        base = (blk // nq) * cls_len
        q0 = (blk % nq) * A_QBLK
        k0 = _window_start(q0, cls_len)
        _band_block(cst, qs, ks, vs, bias_ref, eye_ref, os_, ls_, s_scr, p_scr,
                    base + q0, base + k0, base + q0, (q0 - k0) // A_HALF)
        return carry

    lax.fori_loop(0, n_cls * nq, body, 0, unroll=2)
    for t in range(n_tiles):
        for cl in range(n_cls):
            rows = slice(cl * cls_len + t * c, cl * cls_len + (t + 1) * c)
            o_ref[t, cl] = os_[rows, :]
            lse_ref[t, cl] = ls_[rows, :]


def _attn_a(qkv_a, band_bias, eye2, b, s, gi, dil):
    cls_len = s // dil
    assert cls_len >= A_KWIN and cls_len % A_QBLK == 0 and (cls_len // A_QBLK) % 2 == 0
    stage = [pltpu.VMEM((A_HEADS // 2, 2 * A_QBLK, A_KWIN), F32),
             pltpu.VMEM((A_HEADS // 2, 2 * A_QBLK, A_KWIN), BF16)]
    const_specs = [pl.BlockSpec(band_bias.shape, lambda *_: (0, 0, 0)),
                   pl.BlockSpec(eye2.shape, lambda *_: (0, 0))]
    if dil == 1:
        tq = min(cls_len, 1024)
        view = qkv_a.reshape(b, s, A_QKV)
        kern = functools.partial(_attn_a_natural_kernel, tq=tq, cls_len=cls_len)
        o, lse = pl.pallas_call(
            kern,
            grid=(b, cls_len // tq),
            in_specs=[
                pl.BlockSpec((None, tq, BRANCH_WIDTH), lambda bi, t: (bi, t, 3 * gi)),
                pl.BlockSpec((None, cls_len, BRANCH_WIDTH), lambda bi, t: (bi, 0, 3 * gi + 1)),
                pl.BlockSpec((None, cls_len, BRANCH_WIDTH), lambda bi, t: (bi, 0, 3 * gi + 2)),
            ] + const_specs,
            out_specs=[
                pl.BlockSpec((None, tq, BRANCH_WIDTH), lambda bi, t: (bi, t, 0)),
                pl.BlockSpec((None, tq, LANES), lambda bi, t: (bi, t, 0)),
            ],
            out_shape=[jax.ShapeDtypeStruct((b, s, BRANCH_WIDTH), BF16),
                       jax.ShapeDtypeStruct((b, s, LANES), F32)],
            scratch_shapes=stage,
            compiler_params=_cparams(("parallel", "arbitrary")),
            name=f"attn_a{gi}",
        )(view, view, view, band_bias, eye2)
    else:
        n_tiles = s // PERM_TILE
        c = PERM_TILE // dil
        n_cls = max(1, 1024 // cls_len)
        assert dil % n_cls == 0 and c % 16 == 0
        view = qkv_a.reshape(b, n_tiles, dil, c, A_QKV)
        kern = functools.partial(_attn_a_classes_kernel, n_tiles=n_tiles, n_cls=n_cls, c=c, cls_len=cls_len)
        rows = n_cls * cls_len
        blk = lambda width: (None, n_tiles, n_cls, c, width)
        o, lse = pl.pallas_call(
            kern,
            grid=(b, dil // n_cls),
            in_specs=[
                pl.BlockSpec(blk(BRANCH_WIDTH), lambda bi, r: (bi, 0, r, 0, 3 * gi)),
                pl.BlockSpec(blk(BRANCH_WIDTH), lambda bi, r: (bi, 0, r, 0, 3 * gi + 1)),
                pl.BlockSpec(blk(BRANCH_WIDTH), lambda bi, r: (bi, 0, r, 0, 3 * gi + 2)),
            ] + const_specs,
            out_specs=[
                pl.BlockSpec(blk(BRANCH_WIDTH), lambda bi, r: (bi, 0, r, 0, 0)),
                pl.BlockSpec(blk(LANES), lambda bi, r: (bi, 0, r, 0, 0)),
            ],
            out_shape=[jax.ShapeDtypeStruct((b, n_tiles, dil, c, BRANCH_WIDTH), BF16),
                       jax.ShapeDtypeStruct((b, n_tiles, dil, c, LANES), F32)],
            scratch_shapes=[pltpu.VMEM((rows, BRANCH_WIDTH), BF16)] * 4
                           + [pltpu.VMEM((rows, LANES), F32)] + stage,
            compiler_params=_cparams(("parallel", "arbitrary")),
            name=f"attn_a{gi}",
        )(view, view, view, band_bias, eye2)
    return o.reshape(b * s, BRANCH_WIDTH), lse.reshape(b * s, LANES)


def _attn_b_kernel(q_ref, k_ref, v_ref, o_ref):
    rep = B_HEADS // B_KV_HEADS
    outs = []
    for kh in range(B_KV_HEADS):
        ks = slice(kh * HEAD_DIM, (kh + 1) * HEAD_DIM)
        k = k_ref[:, ks]
        v = v_ref[:, ks]
        for r in range(rep):
            h = kh * rep + r
            q = q_ref[:, h * HEAD_DIM:(h + 1) * HEAD_DIM]
            sc = lax.dot_general(q, k, (((1,), (1,)), ((), ())), preferred_element_type=F32)
            mx = jnp.max(sc, axis=-1, keepdims=True)
            p = jnp.exp(sc - mx)
            l = jnp.sum(p, axis=-1, keepdims=True)
            o = jnp.dot(p.astype(BF16), v, preferred_element_type=F32)
            outs.append(o / l)
    o_ref[...] = jnp.concatenate(outs, axis=1).astype(BF16)


def _attn_b(qkv_b, b, s, tq):
    view = qkv_b.reshape(b, s, B_QKV)
    o = pl.pallas_call(
        _attn_b_kernel,
        grid=(b, s // tq),
        in_specs=[
            pl.BlockSpec((None, tq, B_Q), lambda bi, t: (bi, t, 0)),
            pl.BlockSpec((None, s, B_KV), lambda bi, t: (bi, 0, B_Q // B_KV)),
            pl.BlockSpec((None, s, B_KV), lambda bi, t: (bi, 0, B_Q // B_KV + 1)),
        ],
        out_specs=pl.BlockSpec((None, tq, B_Q), lambda bi, t: (bi, t, 0)),
        out_shape=jax.ShapeDtypeStruct((b, s, B_Q), BF16),
        compiler_params=_cparams(("parallel", "arbitrary")),
        name="attn_b",
    )(view, view, view)
    return o.reshape(b * s, B_Q)


def _attn_m_kernel(q_ref, k_ref, v_ref, o_ref):
    scale = M_HEAD_DIM ** -0.5
    outs = []
    for h in range(M_HEADS):
        hs = slice(h * M_HEAD_DIM, (h + 1) * M_HEAD_DIM)
        sc = lax.dot_general(q_ref[:, hs], k_ref[:, hs], (((1,), (1,)), ((), ())),
                             preferred_element_type=F32) * scale
        mx = jnp.max(sc, axis=-1, keepdims=True)
        p = jnp.exp(sc - mx)
        l = jnp.sum(p, axis=-1, keepdims=True)
        o = jnp.dot(p.astype(BF16), v_ref[:, hs], preferred_element_type=F32)
        outs.append(o / l)
    o_ref[...] = jnp.concatenate(outs, axis=1).astype(BF16)


def _attn_m(mq, kvm, b, s, n_mem, tq):
    qv = mq.reshape(b, s, M_Q)
    kv = kvm.reshape(b, n_mem, 2 * M_Q)
    o = pl.pallas_call(
        _attn_m_kernel,
        grid=(b, s // tq),
        in_specs=[
            pl.BlockSpec((None, tq, M_Q), lambda bi, t: (bi, t, 0)),
            pl.BlockSpec((None, n_mem, M_Q), lambda bi, t: (bi, 0, 0)),
            pl.BlockSpec((None, n_mem, M_Q), lambda bi, t: (bi, 0, 1)),
        ],
        out_specs=pl.BlockSpec((None, tq, M_Q), lambda bi, t: (bi, t, 0)),
        out_shape=jax.ShapeDtypeStruct((b, s, M_Q), BF16),
        compiler_params=_cparams(("parallel", "arbitrary")),
        name="attn_m",
    )(qv, kv, kv)
    return o.reshape(b * s, M_Q)


def _final_kernel(x_ref, oa0_ref, oa1_ref, oa2_ref, l0_ref, l1_ref, l2_ref, ob_ref, om_ref,
                  gpre_ref, wg_ref, bm_ref, wa_ref, wb_ref, wm_ref, wout_ref, gpost_ref,
                  exp_ref, o_ref, nat_o, nat_l, *, tm, dils):
    xf = x_ref[...]
    h = _rms_norm_rows(xf, gpre_ref[...]).astype(BF16)
    n_ch = BRANCH_WIDTH // LANES

    og, lses = [], []
    for gi, (dil, og_ref, l_ref) in enumerate(zip(dils, (oa0_ref, oa1_ref, oa2_ref),
                                                  (l0_ref, l1_ref, l2_ref))):
        if dil == 1:
            o = og_ref[...].astype(F32)
            og.append([o[:, ch * LANES:(ch + 1) * LANES] for ch in range(n_ch)])
            lses.append(l_ref[...])
            continue
        cc = tm // dil
        for r in range(dil):
            o = og_ref[r].astype(F32)
            for ch in range(n_ch):
                nat_o[gi * n_ch + ch, pl.ds(r, cc, stride=dil), :] = o[:, ch * LANES:(ch + 1) * LANES]
            nat_l[gi, pl.ds(r, cc, stride=dil), :] = l_ref[r]
        og.append([nat_o[gi * n_ch + ch] for ch in range(n_ch)])
        lses.append(nat_l[gi])

    mx = functools.reduce(jnp.maximum, lses)
    es = [jnp.exp(l - mx) for l in lses]
    inv_z = 1.0 / functools.reduce(lambda a, c: a + c, es)
    expand = exp_ref[...]
    oa_ch = [None] * n_ch
    for e, o_chunks in zip(es, og):
        w = e * inv_z
        hi = w.astype(BF16)
        lo = (w - hi.astype(F32)).astype(BF16)
        w_wide = (jnp.dot(hi, expand, preferred_element_type=F32)
                  + jnp.dot(lo, expand, preferred_element_type=F32))
        for ch in range(n_ch):
            term = w_wide[:, ch * LANES:(ch + 1) * LANES] * o_chunks[ch]
            oa_ch[ch] = term if oa_ch[ch] is None else oa_ch[ch] + term
    oa = jnp.concatenate(oa_ch, axis=1)

    def branch(o, idx, w_ref):
        gs = slice(idx * BRANCH_WIDTH, (idx + 1) * BRANCH_WIDTH)
        ms = slice(3 * BRANCH_WIDTH + idx * D_MODEL, 3 * BRANCH_WIDTH + (idx + 1) * D_MODEL)
        ms_b = slice(idx * D_MODEL, (idx + 1) * D_MODEL)
        gate = jnp.dot(h, wg_ref[:, gs], preferred_element_type=F32)
        gate = gate * jax.nn.sigmoid(gate)
        y = jnp.dot((o * gate).astype(BF16), w_ref[...], preferred_element_type=F32)
        mg = jnp.dot(h, wg_ref[:, ms], preferred_element_type=F32) + bm_ref[:, ms_b]
        return jax.nn.sigmoid(mg) * y

    merged = branch(oa, 0, wa_ref)
    merged = merged + branch(ob_ref[...].astype(F32), 1, wb_ref)
    merged = merged + branch(om_ref[...].astype(F32), 2, wm_ref)
    out = jnp.dot(merged.astype(BF16), wout_ref[...], preferred_element_type=F32)
    o_ref[...] = xf + _rms_norm_rows(out, gpost_ref[...])


def _final(x2, oas, lses, ob, om, g_pre, w_g, b_merge, w_a, w_b, w_m, w_out, g_post, expand, dils, tm):
    m = x2.shape[0]
    per_tile = PERM_TILE // tm
    row = lambda n: pl.BlockSpec((tm, n), lambda i: (i, 0))
    full = lambda a: pl.BlockSpec(a.shape, lambda i: (0, 0))

    def group_operand(a, dil, width):
        if dil == 1:
            return a, row(width)
        cc = tm // dil
        view = a.reshape(m // PERM_TILE, dil, per_tile, cc, width)
        return view, pl.BlockSpec((None, dil, None, cc, width),
                                  lambda i: (i // per_tile, 0, i % per_tile, 0, 0))

    o_ops = [group_operand(a, dil, BRANCH_WIDTH) for a, dil in zip(oas, dils)]
    l_ops = [group_operand(a, dil, LANES) for a, dil in zip(lses, dils)]
    consts = (g_pre, w_g, b_merge, w_a, w_b, w_m, w_out, g_post, expand)
    kern = functools.partial(_final_kernel, tm=tm, dils=dils)
    n_ch = BRANCH_WIDTH // LANES
    return pl.pallas_call(
        kern,
        grid=(m // tm,),
        in_specs=([row(D_MODEL)] + [spec for _, spec in o_ops] + [spec for _, spec in l_ops]
                  + [row(BRANCH_WIDTH)] * 2 + [full(a) for a in consts]),
        out_specs=row(D_MODEL),
        out_shape=jax.ShapeDtypeStruct((m, D_MODEL), F32),
        scratch_shapes=[pltpu.VMEM((len(dils) * n_ch, tm, LANES), F32),
                        pltpu.VMEM((len(dils), tm, LANES), F32)],
        compiler_params=_cparams(("parallel",)),
        name="final",
    )(x2, *[a for a, _ in o_ops], *[a for a, _ in l_ops], ob, om, *consts)


def _rotary_tables(pos_per_lane, inv_per_lane, is_second_half):
    ang = pos_per_lane * inv_per_lane[None, :]
    cos, sin = jnp.cos(ang), jnp.sin(ang)
    zero = jnp.zeros_like(sin)
    return jnp.stack([cos,
                      jnp.where(is_second_half[None, :], sin, zero),
                      jnp.where(is_second_half[None, :], zero, -sin)]).astype(F32)


def _class_major_positions(s, dil):
    n = np.arange(s)
    within = n % PERM_TILE
    c = PERM_TILE // dil
    return (n // PERM_TILE) * PERM_TILE + (within % c) * dil + within // c


def _tables_a(s, dils):
    half = HEAD_DIM // 2
    lane = np.arange(LANES)
    inv = jnp.power(ROPE_THETA, -jnp.asarray(lane % half, F32) * 2.0 / HEAD_DIM)
    second = jnp.asarray((lane % HEAD_DIM) >= half)
    tables = []
    for dil in dils:
        pos = jnp.asarray(_class_major_positions(s, dil), F32)
        tables.append(_rotary_tables(jnp.broadcast_to(pos[:, None], (s, LANES)), inv, second))
    return jnp.stack(tables)


def _band_tables():
    q = np.arange(A_QBLK)[None, :, None]
    k = np.arange(A_KWIN)[None, None, :]
    delta = (np.arange(3) * A_HALF)[:, None, None]
    bias = np.where(np.abs(q - k + delta) <= A_HALF, 0.0, NEG_INF).astype(np.float32)
    eye2 = np.concatenate([np.eye(A_QBLK), np.eye(A_QBLK)], axis=0)
    return jnp.asarray(bias.transpose(0, 2, 1), BF16), jnp.asarray(eye2, BF16)


def _tables_b(s):
    dr = HEAD_DIM // 2
    half = dr // 2
    lane = np.arange(LANES)
    inv = jnp.power(ROPE_THETA, -jnp.asarray(lane % half, F32) * 2.0 / dr)
    row_pos = jnp.repeat(jnp.arange(s // GRID_W, dtype=F32), GRID_W)
    col_pos = (jnp.arange(s) % GRID_W).astype(F32)
    uses_col = jnp.asarray((lane % HEAD_DIM) >= dr)
    pos = jnp.where(uses_col[None, :], col_pos[:, None], row_pos[:, None])
    return _rotary_tables(pos, inv, jnp.asarray((lane % dr) >= half))


def kernel(x, mem, g_pre, w_in, b_merge, q_norm, k_norm, g_mem, w_mem_kv, w_br_a, w_br_b, w_br_m, w_out, g_post):
    b, s, d = x.shape
    n_mem = mem.shape[1]
    depth = g_pre.shape[0]
    n_groups = len(A_GROUPS)

    dils = tuple(dil for _, dil in A_GROUPS)
    assert s % PERM_TILE == 0
    t_a = _tables_a(s, dils)
    band_bias, eye2 = _band_tables()
    t_b = _tables_b(s)
    lane = np.arange(LANES)
    ones_bd = jnp.asarray((lane[:, None] // HEAD_DIM) == (lane[None, :] // HEAD_DIM), BF16)
    wide = np.arange(BRANCH_WIDTH)
    expand = jnp.asarray(lane[:, None] == (wide[None, :] // HEAD_DIM) * LSE_LANES_PER_HEAD, BF16)

    for layer in range(depth):
        w = w_in[layer]
        w_a = w[:, :A_QKV].reshape(d, 3, n_groups, BRANCH_WIDTH)
        w_a = w_a * jnp.asarray([HEAD_DIM ** -0.5, 1.0, 1.0], F32)[None, :, None, None]
        w_a = w_a.transpose(0, 2, 1, 3).reshape(d, A_QKV).astype(BF16)
        off = A_QKV
        w_bm = w[:, off:off + B_QKV + M_Q].astype(BF16)
        off += B_QKV + M_Q
        w_g = w[:, off:].astype(BF16)
        gq = jnp.tile(q_norm[layer] * HEAD_DIM ** -0.5, LANES // HEAD_DIM)[None, :]
        gk = jnp.tile(k_norm[layer], LANES // HEAD_DIM)[None, :]

        x2 = x.reshape(b * s, d)
        gp = g_pre[layer][None, :]
        qkv_a = _proj_a(x2, gp, w_a, t_a, s, dils)
        qkv_b, mq = _proj_bm(x2, gp, w_bm, t_b, gq, gk, ones_bd, s, tm=512)
        kvm = _mem_kv(mem.reshape(b * n_mem, d), g_mem[layer][None, :],
                      w_mem_kv[layer].astype(BF16), tm=n_mem)

        oas, lses = [], []
        for gi, (_, dil) in enumerate(A_GROUPS):
            o_g, lse_g = _attn_a(qkv_a, band_bias, eye2, b, s, gi, dil)
            oas.append(o_g)
            lses.append(lse_g)
        ob = _attn_b(qkv_b, b, s, tq=256)
        om = _attn_m(mq, kvm, b, s, n_mem, tq=1024)

        x2 = _final(x2, oas, lses, ob, om, gp, w_g, b_merge[layer][None, :],
                    w_br_a[layer].astype(BF16), w_br_b[layer].astype(BF16),
                    w_br_m[layer].astype(BF16), w_out[layer].astype(BF16),
                    g_post[layer][None, :], expand, dils, tm=256)
        x = x2.reshape(b, s, d)
    return x
```

```python
import functools

import jax
import jax.numpy as jnp
import numpy as np
from jax import lax
from jax.experimental import pallas as pl
from jax.experimental.pallas import tpu as pltpu

D_MODEL = 1024
HEAD_DIM = 64
BRANCH_WIDTH = 512
A_GROUPS = ((128, 1), (512, 4), (2048, 16))
A_HEADS = 8
B_HEADS = 8
B_KV_HEADS = 2
M_HEADS = 4
M_HEAD_DIM = 128
GRID_W = 64
ROPE_THETA = 10000.0
NORM_EPS = 1e-6
NEG_INF = -1e30
N_BRANCHES = 3

A_QKV = 3 * len(A_GROUPS) * A_HEADS * HEAD_DIM
B_Q = B_HEADS * HEAD_DIM
B_KV = B_KV_HEADS * HEAD_DIM
B_QKV = B_Q + 2 * B_KV
M_Q = M_HEADS * M_HEAD_DIM
GATE_W = 3 * BRANCH_WIDTH + N_BRANCHES * D_MODEL

LANES = 128
A_QBLK = 128
A_KWIN = 256
A_HALF = 64
LSE_LANES_PER_HEAD = LANES // A_HEADS
PERM_TILE = 1024
ROW_CHUNK = 256
B_ROWS = 256
VMEM_LIMIT = 56 * 1024 * 1024

F32 = jnp.float32
BF16 = jnp.bfloat16


def _cparams(sem):
    return pltpu.CompilerParams(dimension_semantics=sem, vmem_limit_bytes=VMEM_LIMIT)


def _rms_norm_rows(xf, g):
    return xf * lax.rsqrt(jnp.mean(xf * xf, axis=-1, keepdims=True) + NORM_EPS) * g


def _rotate(x, c, s_fwd, s_bwd, shift):
    return (x * c + pltpu.roll(x, shift, axis=1) * s_fwd
            + pltpu.roll(x, LANES - shift, axis=1) * s_bwd)


def _proj_a_kernel(x_ref, g_ref, w_ref, t_ref, o_ref, hf_ref, hp_ref, *, tm, dils):
    j = pl.program_id(1)
    n_lane = D_MODEL // LANES

    @pl.when(j == 0)
    def _():
        h = _rms_norm_rows(x_ref[...], g_ref[...])
        hb = h.astype(BF16)
        for ch in range(n_lane):
            hf_ref[ch] = h[:, ch * LANES:(ch + 1) * LANES]
        for gi, dil in enumerate(dils):
            if dil == 1:
                hp_ref[gi] = hb
                continue
            c = tm // dil
            for r in range(dil):
                for ch in range(n_lane):
                    hp_ref[gi, r * c:(r + 1) * c, ch * LANES:(ch + 1) * LANES] = (
                        hf_ref[ch, pl.ds(r, c, stride=dil), :].astype(BF16))

    n_rot = 2 * BRANCH_WIDTH // LANES
    for rc in range(tm // ROW_CHUNK):
        rows = slice(rc * ROW_CHUNK, (rc + 1) * ROW_CHUNK)
        acc = jnp.dot(hp_ref[j, rows, :], w_ref[...], preferred_element_type=F32)
        c, sf, sb = t_ref[0, rows, :], t_ref[1, rows, :], t_ref[2, rows, :]
        for ch in range(n_rot):
            sl = slice(ch * LANES, (ch + 1) * LANES)
            o_ref[rows, sl] = _rotate(acc[:, sl], c, sf, sb, HEAD_DIM // 2).astype(BF16)
        o_ref[rows, n_rot * LANES:] = acc[:, n_rot * LANES:].astype(BF16)


def _proj_a(x2, g_pre, w_a, t_a, s, dils):
    m = x2.shape[0]
    tm = PERM_TILE
    n_sblk = s // tm
    group_w = 3 * BRANCH_WIDTH
    kern = functools.partial(_proj_a_kernel, tm=tm, dils=dils)
    return pl.pallas_call(
        kern,
        grid=(m // tm, len(dils)),
        in_specs=[
            pl.BlockSpec((tm, D_MODEL), lambda i, j: (i, 0)),
            pl.BlockSpec((1, D_MODEL), lambda i, j: (0, 0)),
            pl.BlockSpec((D_MODEL, group_w), lambda i, j: (0, j)),
            pl.BlockSpec((None, 3, tm, LANES), lambda i, j: (j, 0, i % n_sblk, 0)),
        ],
        out_specs=pl.BlockSpec((tm, group_w), lambda i, j: (i, j)),
        out_shape=jax.ShapeDtypeStruct((m, A_QKV), BF16),
        scratch_shapes=[pltpu.VMEM((D_MODEL // LANES, tm, LANES), F32),
                        pltpu.VMEM((len(dils), tm, D_MODEL), BF16)],
        compiler_params=_cparams(("parallel", "arbitrary")),
        name="proj_a",
    )(x2, g_pre, w_a, t_a)


def _proj_bm_kernel(x_ref, g_ref, w_ref, t_ref, gq_ref, gk_ref, ones_ref, ob_ref, om_ref):
    h = _rms_norm_rows(x_ref[...], g_ref[...]).astype(BF16)
    acc = jnp.dot(h, w_ref[...], preferred_element_type=F32)
    c, sf, sb = t_ref[0], t_ref[1], t_ref[2]
    ones_bd = ones_ref[...]
    n_qk = (B_Q + B_KV) // LANES
    for ch in range(n_qk):
        sl = slice(ch * LANES, (ch + 1) * LANES)
        y = acc[:, sl]
        sq = y * y
        hi = sq.astype(BF16)
        lo = (sq - hi.astype(F32)).astype(BF16)
        ss = (jnp.dot(hi, ones_bd, preferred_element_type=F32)
              + jnp.dot(lo, ones_bd, preferred_element_type=F32))
        gain = gq_ref[...] if ch < B_Q // LANES else gk_ref[...]
        y = y * lax.rsqrt(ss * (1.0 / HEAD_DIM) + NORM_EPS) * gain
        ob_ref[:, sl] = _rotate(y, c, sf, sb, HEAD_DIM // 4).astype(BF16)
    ob_ref[:, B_Q + B_KV:] = acc[:, B_Q + B_KV:B_QKV].astype(BF16)
    om_ref[...] = acc[:, B_QKV:].astype(BF16)


def _proj_bm(x2, g_pre, w_bm, t_b, gq, gk, ones_bd, s, tm):
    m = x2.shape[0]
    n_sblk = s // tm
    n = B_QKV + M_Q
    return pl.pallas_call(
        _proj_bm_kernel,
        grid=(m // tm,),
        in_specs=[
            pl.BlockSpec((tm, D_MODEL), lambda i: (i, 0)),
            pl.BlockSpec((1, D_MODEL), lambda i: (0, 0)),
            pl.BlockSpec((D_MODEL, n), lambda i: (0, 0)),
            pl.BlockSpec((3, tm, LANES), lambda i: (0, i % n_sblk, 0)),
            pl.BlockSpec((1, LANES), lambda i: (0, 0)),
            pl.BlockSpec((1, LANES), lambda i: (0, 0)),
            pl.BlockSpec((LANES, LANES), lambda i: (0, 0)),
        ],
        out_specs=[
            pl.BlockSpec((tm, B_QKV), lambda i: (i, 0)),
            pl.BlockSpec((tm, M_Q), lambda i: (i, 0)),
        ],
        out_shape=[jax.ShapeDtypeStruct((m, B_QKV), BF16),
                   jax.ShapeDtypeStruct((m, M_Q), BF16)],
        compiler_params=_cparams(("parallel",)),
        name="proj_bm",
    )(x2, g_pre, w_bm, t_b, gq, gk, ones_bd)


def _mem_kv_kernel(x_ref, g_ref, w_ref, o_ref):
    h = _rms_norm_rows(x_ref[...], g_ref[...]).astype(BF16)
    o_ref[...] = jnp.dot(h, w_ref[...], preferred_element_type=F32).astype(BF16)


def _mem_kv(mem2, g_mem, w_kv, tm):
    m = mem2.shape[0]
    n = w_kv.shape[1]
    return pl.pallas_call(
        _mem_kv_kernel,
        grid=(m // tm,),
        in_specs=[
            pl.BlockSpec((tm, D_MODEL), lambda i: (i, 0)),
            pl.BlockSpec((1, D_MODEL), lambda i: (0, 0)),
            pl.BlockSpec((D_MODEL, n), lambda i: (0, 0)),
        ],
        out_specs=pl.BlockSpec((tm, n), lambda i: (i, 0)),
        out_shape=jax.ShapeDtypeStruct((m, n), BF16),
        compiler_params=_cparams(("parallel",)),
        name="mem_kv",
    )(mem2, g_mem, w_kv)


def _band_consts():
    lane = lax.broadcasted_iota(jnp.int32, (A_QBLK, LANES), 1)
    return dict(first_half=lane < HEAD_DIM, head_of_lane=lane // LSE_LANES_PER_HEAD,
                ones=jnp.ones((A_KWIN, LANES), BF16))


def _band_block(cst, q_ref, k_ref, v_ref, bias_ref, eye_ref, o_ref, lse_ref, s_scr, p_scr,
                q_row, k_row, out_row, band_type):
    first_half, head_of_lane, ones = cst["first_half"], cst["head_of_lane"], cst["ones"]
    q_row = pl.multiple_of(q_row, A_QBLK)
    k_row = pl.multiple_of(k_row, A_HALF)
    out_row = pl.multiple_of(out_row, A_QBLK)
    bias_t = bias_ref[band_type]
    eye2 = eye_ref[...]
    n_pair = A_HEADS // 2
    nt = (((1,), (1,)), ((), ()))
    for j in range(n_pair):
        ps = slice(j * LANES, (j + 1) * LANES)
        qp = q_ref[pl.ds(q_row, A_QBLK), ps]
        kp = k_ref[pl.ds(k_row, A_KWIN), ps]
        zero = jnp.zeros_like(qp)
        q2 = jnp.concatenate([jnp.where(first_half, qp, zero), jnp.where(first_half, zero, qp)], axis=0)
        s_scr[j] = lax.dot_general(jnp.concatenate([q2, eye2], axis=1),
                                   jnp.concatenate([kp, bias_t], axis=1), nt,
                                   preferred_element_type=F32)
    mxs = []
    for j in range(n_pair):
        sc = s_scr[j]
        mx = jnp.max(sc, axis=-1, keepdims=True)
        p_scr[j] = jnp.exp(sc - mx).astype(BF16)
        mxs.append(mx)
    m_tile = jnp.zeros((A_QBLK, LANES), F32)
    l_tile = jnp.ones((A_QBLK, LANES), F32)
    for j in range(n_pair):
        ps = slice(j * LANES, (j + 1) * LANES)
        vp = v_ref[pl.ds(k_row, A_KWIN), ps]
        ov = jnp.dot(p_scr[j], jnp.concatenate([vp, ones], axis=1), preferred_element_type=F32)
        o_sel = jnp.where(first_half, ov[:A_QBLK, :LANES], ov[A_QBLK:, :LANES])
        l_sel = jnp.where(first_half, ov[:A_QBLK, LANES:], ov[A_QBLK:, LANES:])
        o_ref[pl.ds(out_row, A_QBLK), ps] = (o_sel / l_sel).astype(BF16)
        for e in range(2):
            rs = slice(e * A_QBLK, (e + 1) * A_QBLK)
            m_tile = jnp.where(head_of_lane == 2 * j + e, mxs[j][rs], m_tile)
            l_tile = jnp.where(head_of_lane == 2 * j + e, ov[rs, LANES:], l_tile)
    lse_ref[pl.ds(out_row, A_QBLK), :] = m_tile + jnp.log(l_tile)


def _window_start(q0, cls_len):
    return jnp.clip(q0 - A_HALF, 0, cls_len - A_KWIN)


def _attn_a_natural_kernel(q_ref, k_ref, v_ref, bias_ref, eye_ref, o_ref, lse_ref, s_scr, p_scr,
                           *, tq, cls_len):
    t = pl.program_id(1)
    cst = _band_consts()

    def body(qi, carry):
        q0 = t * tq + qi * A_QBLK
        k0 = _window_start(q0, cls_len)
        _band_block(cst, q_ref, k_ref, v_ref, bias_ref, eye_ref, o_ref, lse_ref, s_scr, p_scr,
                    qi * A_QBLK, k0, qi * A_QBLK, (q0 - k0) // A_HALF)
        return carry

    lax.fori_loop(0, tq // A_QBLK, body, 0, unroll=2)


def _attn_a_classes_kernel(q_ref, k_ref, v_ref, bias_ref, eye_ref, o_ref, lse_ref,
                           qs, ks, vs, os_, ls_, s_scr, p_scr, *, n_tiles, n_cls, c, cls_len):
    for t in range(n_tiles):
        for cl in range(n_cls):
            rows = slice(cl * cls_len + t * c, cl * cls_len + (t + 1) * c)
            qs[rows, :] = q_ref[t, cl]
            ks[rows, :] = k_ref[t, cl]
            vs[rows, :] = v_ref[t, cl]
    cst = _band_consts()
    nq = cls_len // A_QBLK

    def body(blk, carry):
        base = (blk // nq) * cls_len
        q0 = (blk % nq) * A_QBLK
        k0 = _window_start(q0, cls_len)
        _band_block(cst, qs, ks, vs, bias_ref, eye_ref, os_, ls_, s_scr, p_scr,
                    base + q0, base + k0, base + q0, (q0 - k0) // A_HALF)
        return carry

    lax.fori_loop(0, n_cls * nq, body, 0, unroll=2)
    for t in range(n_tiles):
        for cl in range(n_cls):
            rows = slice(cl * cls_len + t * c, cl * cls_len + (t + 1) * c)
            o_ref[t, cl] = os_[rows, :]
            lse_ref[t, cl] = ls_[rows, :]


def _attn_a(qkv_a, band_bias, eye2, b, s, gi, dil):
    cls_len = s // dil
    assert cls_len >= A_KWIN and cls_len % A_QBLK == 0 and (cls_len // A_QBLK) % 2 == 0
    stage = [pltpu.VMEM((A_HEADS // 2, 2 * A_QBLK, A_KWIN), F32),
             pltpu.VMEM((A_HEADS // 2, 2 * A_QBLK, A_KWIN), BF16)]
    const_specs = [pl.BlockSpec(band_bias.shape, lambda *_: (0, 0, 0)),
                   pl.BlockSpec(eye2.shape, lambda *_: (0, 0))]
    if dil == 1:
        tq = min(cls_len, 1024)
        view = qkv_a.reshape(b, s, A_QKV)
        kern = functools.partial(_attn_a_natural_kernel, tq=tq, cls_len=cls_len)
        o, lse = pl.pallas_call(
            kern,
            grid=(b, cls_len // tq),
            in_specs=[
                pl.BlockSpec((None, tq, BRANCH_WIDTH), lambda bi, t: (bi, t, 3 * gi)),
                pl.BlockSpec((None, cls_len, BRANCH_WIDTH), lambda bi, t: (bi, 0, 3 * gi + 1)),
                pl.BlockSpec((None, cls_len, BRANCH_WIDTH), lambda bi, t: (bi, 0, 3 * gi + 2)),
            ] + const_specs,
            out_specs=[
                pl.BlockSpec((None, tq, BRANCH_WIDTH), lambda bi, t: (bi, t, 0)),
                pl.BlockSpec((None, tq, LANES), lambda bi, t: (bi, t, 0)),
            ],
            out_shape=[jax.ShapeDtypeStruct((b, s, BRANCH_WIDTH), BF16),
                       jax.ShapeDtypeStruct((b, s, LANES), F32)],
            scratch_shapes=stage,
            compiler_params=_cparams(("parallel", "arbitrary")),
            name=f"attn_a{gi}",
        )(view, view, view, band_bias, eye2)
    else:
        n_tiles = s // PERM_TILE
        c = PERM_TILE // dil
        n_cls = max(1, 1024 // cls_len)
        assert dil % n_cls == 0 and c % 16 == 0
        view = qkv_a.reshape(b, n_tiles, dil, c, A_QKV)
        kern = functools.partial(_attn_a_classes_kernel, n_tiles=n_tiles, n_cls=n_cls, c=c, cls_len=cls_len)
        rows = n_cls * cls_len
        blk = lambda width: (None, n_tiles, n_cls, c, width)
        o, lse = pl.pallas_call(
            kern,
            grid=(b, dil // n_cls),
            in_specs=[
                pl.BlockSpec(blk(BRANCH_WIDTH), lambda bi, r: (bi, 0, r, 0, 3 * gi)),
                pl.BlockSpec(blk(BRANCH_WIDTH), lambda bi, r: (bi, 0, r, 0, 3 * gi + 1)),
                pl.BlockSpec(blk(BRANCH_WIDTH), lambda bi, r: (bi, 0, r, 0, 3 * gi + 2)),
            ] + const_specs,
            out_specs=[
                pl.BlockSpec(blk(BRANCH_WIDTH), lambda bi, r: (bi, 0, r, 0, 0)),
                pl.BlockSpec(blk(LANES), lambda bi, r: (bi, 0, r, 0, 0)),
            ],
            out_shape=[jax.ShapeDtypeStruct((b, n_tiles, dil, c, BRANCH_WIDTH), BF16),
                       jax.ShapeDtypeStruct((b, n_tiles, dil, c, LANES), F32)],
            scratch_shapes=[pltpu.VMEM((rows, BRANCH_WIDTH), BF16)] * 4
                           + [pltpu.VMEM((rows, LANES), F32)] + stage,
            compiler_params=_cparams(("parallel", "arbitrary")),
            name=f"attn_a{gi}",
        )(view, view, view, band_bias, eye2)
    return o.reshape(b * s, BRANCH_WIDTH), lse.reshape(b * s, LANES)


def _attn_b_kernel(q_ref, k_ref, v_ref, o_ref, kdup, vext, *, tq):
    s = k_ref.shape[0]

    @pl.when(pl.program_id(1) == 0)
    def _():
        k = k_ref[...]
        v = v_ref[...]
        ones = jnp.ones((s, LANES), BF16)
        for kh in range(B_KV_HEADS):
            ks = slice(kh * HEAD_DIM, (kh + 1) * HEAD_DIM)
            kdup[kh] = jnp.concatenate([k[:, ks], k[:, ks]], axis=1)
            vext[kh] = jnp.concatenate([v[:, ks], v[:, ks], ones], axis=1)

    sub = B_ROWS
    first_half = lax.broadcasted_iota(jnp.int32, (sub, LANES), 1) < HEAD_DIM
    pairs_per_kv = B_HEADS // B_KV_HEADS // 2
    nt = (((1,), (1,)), ((), ()))
    for hh in range(tq // sub):
        rows = slice(hh * sub, (hh + 1) * sub)
        for j in range(B_HEADS // 2):
            kh = j // pairs_per_kv
            ps = slice(j * LANES, (j + 1) * LANES)
            qp = q_ref[rows, ps]
            zero = jnp.zeros_like(qp)
            q2 = jnp.concatenate([jnp.where(first_half, qp, zero), jnp.where(first_half, zero, qp)], axis=0)
            sc = lax.dot_general(q2, kdup[kh], nt, preferred_element_type=F32)
            mx = jnp.max(sc, axis=-1, keepdims=True)
            p = jnp.exp(sc - mx).astype(BF16)
            ov = jnp.dot(p, vext[kh], preferred_element_type=F32)
            o_sel = jnp.where(first_half, ov[:sub, :LANES], ov[sub:, :LANES])
            l_sel = jnp.where(first_half, ov[:sub, LANES:], ov[sub:, LANES:])
            o_ref[rows, ps] = (o_sel / l_sel).astype(BF16)


def _attn_b(qkv_b, b, s, tq):
    view = qkv_b.reshape(b, s, B_QKV)
    o = pl.pallas_call(
        functools.partial(_attn_b_kernel, tq=tq),
        grid=(b, s // tq),
        in_specs=[
            pl.BlockSpec((None, tq, B_Q), lambda bi, t: (bi, t, 0)),
            pl.BlockSpec((None, s, B_KV), lambda bi, t: (bi, 0, B_Q // B_KV)),
            pl.BlockSpec((None, s, B_KV), lambda bi, t: (bi, 0, B_Q // B_KV + 1)),
        ],
        out_specs=pl.BlockSpec((None, tq, B_Q), lambda bi, t: (bi, t, 0)),
        out_shape=jax.ShapeDtypeStruct((b, s, B_Q), BF16),
        scratch_shapes=[pltpu.VMEM((B_KV_HEADS, s, LANES), BF16),
                        pltpu.VMEM((B_KV_HEADS, s, 2 * LANES), BF16)],
        compiler_params=_cparams(("parallel", "arbitrary")),
        name="attn_b",
    )(view, view, view)
    return o.reshape(b * s, B_Q)


def _attn_m_kernel(q_ref, k_ref, v_ref, o_ref):
    scale = M_HEAD_DIM ** -0.5
    outs = []
    for h in range(M_HEADS):
        hs = slice(h * M_HEAD_DIM, (h + 1) * M_HEAD_DIM)
        sc = lax.dot_general(q_ref[:, hs], k_ref[:, hs], (((1,), (1,)), ((), ())),
                             preferred_element_type=F32) * scale
        mx = jnp.max(sc, axis=-1, keepdims=True)
        p = jnp.exp(sc - mx)
        l = jnp.sum(p, axis=-1, keepdims=True)
        o = jnp.dot(p.astype(BF16), v_ref[:, hs], preferred_element_type=F32)
        outs.append(o / l)
    o_ref[...] = jnp.concatenate(outs, axis=1).astype(BF16)


def _attn_m(mq, kvm, b, s, n_mem, tq):
    qv = mq.reshape(b, s, M_Q)
    kv = kvm.reshape(b, n_mem, 2 * M_Q)
    o = pl.pallas_call(
        _attn_m_kernel,
        grid=(b, s // tq),
        in_specs=[
            pl.BlockSpec((None, tq, M_Q), lambda bi, t: (bi, t, 0)),
            pl.BlockSpec((None, n_mem, M_Q), lambda bi, t: (bi, 0, 0)),
            pl.BlockSpec((None, n_mem, M_Q), lambda bi, t: (bi, 0, 1)),
        ],
        out_specs=pl.BlockSpec((None, tq, M_Q), lambda bi, t: (bi, t, 0)),
        out_shape=jax.ShapeDtypeStruct((b, s, M_Q), BF16),
        compiler_params=_cparams(("parallel", "arbitrary")),
        name="attn_m",
    )(qv, kv, kv)
    return o.reshape(b * s, M_Q)


def _final_kernel(x_ref, oa0_ref, oa1_ref, oa2_ref, l0_ref, l1_ref, l2_ref, ob_ref, om_ref,
                  gpre_ref, wg_ref, bm_ref, wa_ref, wb_ref, wm_ref, wout_ref, gpost_ref,
                  exp_ref, o_ref, nat_o, nat_l, *, tm, dils):
    xf = x_ref[...]
    h = _rms_norm_rows(xf, gpre_ref[...]).astype(BF16)
    n_ch = BRANCH_WIDTH // LANES

    og, lses = [], []
    for gi, (dil, og_ref, l_ref) in enumerate(zip(dils, (oa0_ref, oa1_ref, oa2_ref),
                                                  (l0_ref, l1_ref, l2_ref))):
        if dil == 1:
            o = og_ref[...].astype(F32)
            og.append([o[:, ch * LANES:(ch + 1) * LANES] for ch in range(n_ch)])
            lses.append(l_ref[...])
            continue
        cc = tm // dil
        for r in range(dil):
            o = og_ref[r].astype(F32)
            for ch in range(n_ch):
                nat_o[gi * n_ch + ch, pl.ds(r, cc, stride=dil), :] = o[:, ch * LANES:(ch + 1) * LANES]
            nat_l[gi, pl.ds(r, cc, stride=dil), :] = l_ref[r]
        og.append([nat_o[gi * n_ch + ch] for ch in range(n_ch)])
        lses.append(nat_l[gi])

    mx = functools.reduce(jnp.maximum, lses)
    es = [jnp.exp(l - mx) for l in lses]
    inv_z = 1.0 / functools.reduce(lambda a, c: a + c, es)
    expand = exp_ref[...]
    oa_ch = [None] * n_ch
    for e, o_chunks in zip(es, og):
        w = e * inv_z
        hi = w.astype(BF16)
        lo = (w - hi.astype(F32)).astype(BF16)
        w_wide = (jnp.dot(hi, expand, preferred_element_type=F32)
                  + jnp.dot(lo, expand, preferred_element_type=F32))
        for ch in range(n_ch):
            term = w_wide[:, ch * LANES:(ch + 1) * LANES] * o_chunks[ch]
            oa_ch[ch] = term if oa_ch[ch] is None else oa_ch[ch] + term
    oa = jnp.concatenate(oa_ch, axis=1)

    def branch(o, idx, w_ref):
        gs = slice(idx * BRANCH_WIDTH, (idx + 1) * BRANCH_WIDTH)
        ms = slice(3 * BRANCH_WIDTH + idx * D_MODEL, 3 * BRANCH_WIDTH + (idx + 1) * D_MODEL)
        ms_b = slice(idx * D_MODEL, (idx + 1) * D_MODEL)
        gate = jnp.dot(h, wg_ref[:, gs], preferred_element_type=F32)
        gate = gate * jax.nn.sigmoid(gate)
        y = jnp.dot((o * gate).astype(BF16), w_ref[...], preferred_element_type=F32)
        mg = jnp.dot(h, wg_ref[:, ms], preferred_element_type=F32) + bm_ref[:, ms_b]
        return jax.nn.sigmoid(mg) * y

    merged = branch(oa, 0, wa_ref)
    merged = merged + branch(ob_ref[...].astype(F32), 1, wb_ref)
    merged = merged + branch(om_ref[...].astype(F32), 2, wm_ref)
    out = jnp.dot(merged.astype(BF16), wout_ref[...], preferred_element_type=F32)
    o_ref[...] = xf + _rms_norm_rows(out, gpost_ref[...])


def _final(x2, oas, lses, ob, om, g_pre, w_g, b_merge, w_a, w_b, w_m, w_out, g_post, expand, dils, tm):
    m = x2.shape[0]
    per_tile = PERM_TILE // tm
    row = lambda n: pl.BlockSpec((tm, n), lambda i: (i, 0))
    full = lambda a: pl.BlockSpec(a.shape, lambda i: (0, 0))

    def group_operand(a, dil, width):
        if dil == 1:
            return a, row(width)
        cc = tm // dil
        view = a.reshape(m // PERM_TILE, dil, per_tile, cc, width)
        return view, pl.BlockSpec((None, dil, None, cc, width),
                                  lambda i: (i // per_tile, 0, i % per_tile, 0, 0))

    o_ops = [group_operand(a, dil, BRANCH_WIDTH) for a, dil in zip(oas, dils)]
    l_ops = [group_operand(a, dil, LANES) for a, dil in zip(lses, dils)]
    consts = (g_pre, w_g, b_merge, w_a, w_b, w_m, w_out, g_post, expand)
    kern = functools.partial(_final_kernel, tm=tm, dils=dils)
    n_ch = BRANCH_WIDTH // LANES
    return pl.pallas_call(
        kern,
        grid=(m // tm,),
        in_specs=([row(D_MODEL)] + [spec for _, spec in o_ops] + [spec for _, spec in l_ops]
                  + [row(BRANCH_WIDTH)] * 2 + [full(a) for a in consts]),
        out_specs=row(D_MODEL),
        out_shape=jax.ShapeDtypeStruct((m, D_MODEL), F32),
        scratch_shapes=[pltpu.VMEM((len(dils) * n_ch, tm, LANES), F32),
                        pltpu.VMEM((len(dils), tm, LANES), F32)],
        compiler_params=_cparams(("parallel",)),
        name="final",
    )(x2, *[a for a, _ in o_ops], *[a for a, _ in l_ops], ob, om, *consts)


def _rotary_tables(pos_per_lane, inv_per_lane, is_second_half):
    ang = pos_per_lane * inv_per_lane[None, :]
    cos, sin = jnp.cos(ang), jnp.sin(ang)
    zero = jnp.zeros_like(sin)
    return jnp.stack([cos,
                      jnp.where(is_second_half[None, :], sin, zero),
                      jnp.where(is_second_half[None, :], zero, -sin)]).astype(F32)


def _class_major_positions(s, dil):
    n = np.arange(s)
    within = n % PERM_TILE
    c = PERM_TILE // dil
    return (n // PERM_TILE) * PERM_TILE + (within % c) * dil + within // c


def _tables_a(s, dils):
    half = HEAD_DIM // 2
    lane = np.arange(LANES)
    inv = jnp.power(ROPE_THETA, -jnp.asarray(lane % half, F32) * 2.0 / HEAD_DIM)
    second = jnp.asarray((lane % HEAD_DIM) >= half)
    tables = []
    for dil in dils:
        pos = jnp.asarray(_class_major_positions(s, dil), F32)
        tables.append(_rotary_tables(jnp.broadcast_to(pos[:, None], (s, LANES)), inv, second))
    return jnp.stack(tables)


def _band_tables():
    q = np.arange(A_QBLK)[None, :, None]
    k = np.arange(A_KWIN)[None, None, :]
    delta = (np.arange(3) * A_HALF)[:, None, None]
    bias = np.where(np.abs(q - k + delta) <= A_HALF, 0.0, NEG_INF).astype(np.float32)
    eye2 = np.concatenate([np.eye(A_QBLK), np.eye(A_QBLK)], axis=0)
    return jnp.asarray(bias.transpose(0, 2, 1), BF16), jnp.asarray(eye2, BF16)


def _tables_b(s):
    dr = HEAD_DIM // 2
    half = dr // 2
    lane = np.arange(LANES)
    inv = jnp.power(ROPE_THETA, -jnp.asarray(lane % half, F32) * 2.0 / dr)
    row_pos = jnp.repeat(jnp.arange(s // GRID_W, dtype=F32), GRID_W)
    col_pos = (jnp.arange(s) % GRID_W).astype(F32)
    uses_col = jnp.asarray((lane % HEAD_DIM) >= dr)
    pos = jnp.where(uses_col[None, :], col_pos[:, None], row_pos[:, None])
    return _rotary_tables(pos, inv, jnp.asarray((lane % dr) >= half))


def kernel(x, mem, g_pre, w_in, b_merge, q_norm, k_norm, g_mem, w_mem_kv, w_br_a, w_br_b, w_br_m, w_out, g_post):
    b, s, d = x.shape
    n_mem = mem.shape[1]
    depth = g_pre.shape[0]
    n_groups = len(A_GROUPS)

    dils = tuple(dil for _, dil in A_GROUPS)
    assert s % PERM_TILE == 0
    t_a = _tables_a(s, dils)
    band_bias, eye2 = _band_tables()
    t_b = _tables_b(s)
    lane = np.arange(LANES)
    ones_bd = jnp.asarray((lane[:, None] // HEAD_DIM) == (lane[None, :] // HEAD_DIM), BF16)
    wide = np.arange(BRANCH_WIDTH)
    expand = jnp.asarray(lane[:, None] == (wide[None, :] // HEAD_DIM) * LSE_LANES_PER_HEAD, BF16)

    for layer in range(depth):
        w = w_in[layer]
        w_a = w[:, :A_QKV].reshape(d, 3, n_groups, BRANCH_WIDTH)
        w_a = w_a * jnp.asarray([HEAD_DIM ** -0.5, 1.0, 1.0], F32)[None, :, None, None]
        w_a = w_a.transpose(0, 2, 1, 3).reshape(d, A_QKV).astype(BF16)
        off = A_QKV
        w_bm = w[:, off:off + B_QKV + M_Q].astype(BF16)
        off += B_QKV + M_Q
        w_g = w[:, off:].astype(BF16)
        gq = jnp.tile(q_norm[layer] * HEAD_DIM ** -0.5, LANES // HEAD_DIM)[None, :]
        gk = jnp.tile(k_norm[layer], LANES // HEAD_DIM)[None, :]

        x2 = x.reshape(b * s, d)
        gp = g_pre[layer][None, :]
        qkv_a = _proj_a(x2, gp, w_a, t_a, s, dils)
        qkv_b, mq = _proj_bm(x2, gp, w_bm, t_b, gq, gk, ones_bd, s, tm=512)
        kvm = _mem_kv(mem.reshape(b * n_mem, d), g_mem[layer][None, :],
                      w_mem_kv[layer].astype(BF16), tm=n_mem)

        oas, lses = [], []
        for gi, (_, dil) in enumerate(A_GROUPS):
            o_g, lse_g = _attn_a(qkv_a, band_bias, eye2, b, s, gi, dil)
            oas.append(o_g)
            lses.append(lse_g)
        ob = _attn_b(qkv_b, b, s, tq=512)
        om = _attn_m(mq, kvm, b, s, n_mem, tq=1024)

        x2 = _final(x2, oas, lses, ob, om, gp, w_g, b_merge[layer][None, :],
                    w_br_a[layer].astype(BF16), w_br_b[layer].astype(BF16),
                    w_br_m[layer].astype(BF16), w_out[layer].astype(BF16),
                    g_post[layer][None, :], expand, dils, tm=256)
        x = x2.reshape(b, s, d)
    return x
```

```python
import functools

import jax
import jax.numpy as jnp
import numpy as np
from jax import lax
from jax.experimental import pallas as pl
from jax.experimental.pallas import tpu as pltpu

D_MODEL = 1024
HEAD_DIM = 64
BRANCH_WIDTH = 512
A_GROUPS = ((128, 1), (512, 4), (2048, 16))
A_HEADS = 8
B_HEADS = 8
B_KV_HEADS = 2
M_HEADS = 4
M_HEAD_DIM = 128
GRID_W = 64
ROPE_THETA = 10000.0
NORM_EPS = 1e-6
NEG_INF = -1e30
N_BRANCHES = 3

A_QKV = 3 * len(A_GROUPS) * A_HEADS * HEAD_DIM
B_Q = B_HEADS * HEAD_DIM
B_KV = B_KV_HEADS * HEAD_DIM
B_QKV = B_Q + 2 * B_KV
M_Q = M_HEADS * M_HEAD_DIM
GATE_W = 3 * BRANCH_WIDTH + N_BRANCHES * D_MODEL

LANES = 128
A_QBLK = 128
A_KWIN = 256
A_HALF = 64
LSE_LANES_PER_HEAD = LANES // A_HEADS
PERM_TILE = 1024
ROW_CHUNK = 256
B_ROWS = 256
FINAL_ROWS = 256
VMEM_LIMIT = 56 * 1024 * 1024

F32 = jnp.float32
BF16 = jnp.bfloat16


def _cparams(sem):
    return pltpu.CompilerParams(dimension_semantics=sem, vmem_limit_bytes=VMEM_LIMIT)


def _rms_norm_rows(xf, g):
    return xf * lax.rsqrt(jnp.mean(xf * xf, axis=-1, keepdims=True) + NORM_EPS) * g


def _rotate(x, c, s_fwd, s_bwd, shift):
    return (x * c + pltpu.roll(x, shift, axis=1) * s_fwd
            + pltpu.roll(x, LANES - shift, axis=1) * s_bwd)


def _proj_a_kernel(x_ref, g_ref, w_ref, t_ref, o_ref, hf_ref, hp_ref, *, tm, dils):
    j = pl.program_id(1)
    n_lane = D_MODEL // LANES

    @pl.when(j == 0)
    def _():
        h = _rms_norm_rows(x_ref[...], g_ref[...])
        hb = h.astype(BF16)
        for ch in range(n_lane):
            hf_ref[ch] = h[:, ch * LANES:(ch + 1) * LANES]
        for gi, dil in enumerate(dils):
            if dil == 1:
                hp_ref[gi] = hb
                continue
            c = tm // dil
            for r in range(dil):
                for ch in range(n_lane):
                    hp_ref[gi, r * c:(r + 1) * c, ch * LANES:(ch + 1) * LANES] = (
                        hf_ref[ch, pl.ds(r, c, stride=dil), :].astype(BF16))

    n_rot = 2 * BRANCH_WIDTH // LANES
    for rc in range(tm // ROW_CHUNK):
        rows = slice(rc * ROW_CHUNK, (rc + 1) * ROW_CHUNK)
        acc = jnp.dot(hp_ref[j, rows, :], w_ref[...], preferred_element_type=F32)
        c, sf, sb = t_ref[0, rows, :], t_ref[1, rows, :], t_ref[2, rows, :]
        for ch in range(n_rot):
            sl = slice(ch * LANES, (ch + 1) * LANES)
            o_ref[rows, sl] = _rotate(acc[:, sl], c, sf, sb, HEAD_DIM // 2).astype(BF16)
        o_ref[rows, n_rot * LANES:] = acc[:, n_rot * LANES:].astype(BF16)


def _proj_a(x2, g_pre, w_a, t_a, s, dils):
    m = x2.shape[0]
    tm = PERM_TILE
    n_sblk = s // tm
    group_w = 3 * BRANCH_WIDTH
    kern = functools.partial(_proj_a_kernel, tm=tm, dils=dils)
    return pl.pallas_call(
        kern,
        grid=(m // tm, len(dils)),
        in_specs=[
            pl.BlockSpec((tm, D_MODEL), lambda i, j: (i, 0)),
            pl.BlockSpec((1, D_MODEL), lambda i, j: (0, 0)),
            pl.BlockSpec((D_MODEL, group_w), lambda i, j: (0, j)),
            pl.BlockSpec((None, 3, tm, LANES), lambda i, j: (j, 0, i % n_sblk, 0)),
        ],
        out_specs=pl.BlockSpec((tm, group_w), lambda i, j: (i, j)),
        out_shape=jax.ShapeDtypeStruct((m, A_QKV), BF16),
        scratch_shapes=[pltpu.VMEM((D_MODEL // LANES, tm, LANES), F32),
                        pltpu.VMEM((len(dils), tm, D_MODEL), BF16)],
        compiler_params=_cparams(("parallel", "arbitrary")),
        name="proj_a",
    )(x2, g_pre, w_a, t_a)


def _proj_bm_kernel(x_ref, g_ref, w_ref, t_ref, gq_ref, gk_ref, ones_ref, ob_ref, om_ref):
    ones_bd = ones_ref[...]
    n_qk = (B_Q + B_KV) // LANES
    for rc in range(x_ref.shape[0] // ROW_CHUNK):
        rows = slice(rc * ROW_CHUNK, (rc + 1) * ROW_CHUNK)
        h = _rms_norm_rows(x_ref[rows, :], g_ref[...]).astype(BF16)
        acc = jnp.dot(h, w_ref[...], preferred_element_type=F32)
        c, sf, sb = t_ref[0, rows, :], t_ref[1, rows, :], t_ref[2, rows, :]
        for ch in range(n_qk):
            sl = slice(ch * LANES, (ch + 1) * LANES)
            y = acc[:, sl]
            sq = y * y
            hi = sq.astype(BF16)
            lo = (sq - hi.astype(F32)).astype(BF16)
            ss = (jnp.dot(hi, ones_bd, preferred_element_type=F32)
                  + jnp.dot(lo, ones_bd, preferred_element_type=F32))
            gain = gq_ref[...] if ch < B_Q // LANES else gk_ref[...]
            y = y * lax.rsqrt(ss * (1.0 / HEAD_DIM) + NORM_EPS) * gain
            ob_ref[rows, sl] = _rotate(y, c, sf, sb, HEAD_DIM // 4).astype(BF16)
        ob_ref[rows, B_Q + B_KV:] = acc[:, B_Q + B_KV:B_QKV].astype(BF16)
        om_ref[rows, :] = acc[:, B_QKV:].astype(BF16)


def _proj_bm(x2, g_pre, w_bm, t_b, gq, gk, ones_bd, s, tm):
    m = x2.shape[0]
    n_sblk = s // tm
    n = B_QKV + M_Q
    return pl.pallas_call(
        _proj_bm_kernel,
        grid=(m // tm,),
        in_specs=[
            pl.BlockSpec((tm, D_MODEL), lambda i: (i, 0)),
            pl.BlockSpec((1, D_MODEL), lambda i: (0, 0)),
            pl.BlockSpec((D_MODEL, n), lambda i: (0, 0)),
            pl.BlockSpec((3, tm, LANES), lambda i: (0, i % n_sblk, 0)),
            pl.BlockSpec((1, LANES), lambda i: (0, 0)),
            pl.BlockSpec((1, LANES), lambda i: (0, 0)),
            pl.BlockSpec((LANES, LANES), lambda i: (0, 0)),
        ],
        out_specs=[
            pl.BlockSpec((tm, B_QKV), lambda i: (i, 0)),
            pl.BlockSpec((tm, M_Q), lambda i: (i, 0)),
        ],
        out_shape=[jax.ShapeDtypeStruct((m, B_QKV), BF16),
                   jax.ShapeDtypeStruct((m, M_Q), BF16)],
        compiler_params=_cparams(("parallel",)),
        name="proj_bm",
    )(x2, g_pre, w_bm, t_b, gq, gk, ones_bd)


def _mem_kv_kernel(x_ref, g_ref, w_ref, o_ref):
    h = _rms_norm_rows(x_ref[...], g_ref[...]).astype(BF16)
    o_ref[...] = jnp.dot(h, w_ref[...], preferred_element_type=F32).astype(BF16)


def _mem_kv(mem2, g_mem, w_kv, tm):
    m = mem2.shape[0]
    n = w_kv.shape[1]
    return pl.pallas_call(
        _mem_kv_kernel,
        grid=(m // tm,),
        in_specs=[
            pl.BlockSpec((tm, D_MODEL), lambda i: (i, 0)),
            pl.BlockSpec((1, D_MODEL), lambda i: (0, 0)),
            pl.BlockSpec((D_MODEL, n), lambda i: (0, 0)),
        ],
        out_specs=pl.BlockSpec((tm, n), lambda i: (i, 0)),
        out_shape=jax.ShapeDtypeStruct((m, n), BF16),
        compiler_params=_cparams(("parallel",)),
        name="mem_kv",
    )(mem2, g_mem, w_kv)


def _band_consts():
    lane = lax.broadcasted_iota(jnp.int32, (A_QBLK, LANES), 1)
    return dict(first_half=lane < HEAD_DIM, head_of_lane=lane // LSE_LANES_PER_HEAD,
                ones=jnp.ones((A_KWIN, LANES), BF16))


def _band_block(cst, q_ref, k_ref, v_ref, bias_ref, eye_ref, o_ref, lse_ref, s_scr, p_scr,
                q_row, k_row, out_row, band_type):
    first_half, head_of_lane, ones = cst["first_half"], cst["head_of_lane"], cst["ones"]
    q_row = pl.multiple_of(q_row, A_QBLK)
    k_row = pl.multiple_of(k_row, A_HALF)
    out_row = pl.multiple_of(out_row, A_QBLK)
    bias_t = bias_ref[band_type]
    eye2 = eye_ref[...]
    n_pair = A_HEADS // 2
    nt = (((1,), (1,)), ((), ()))
    for j in range(n_pair):
        ps = slice(j * LANES, (j + 1) * LANES)
        qp = q_ref[pl.ds(q_row, A_QBLK), ps]
        kp = k_ref[pl.ds(k_row, A_KWIN), ps]
        zero = jnp.zeros_like(qp)
        q2 = jnp.concatenate([jnp.where(first_half, qp, zero), jnp.where(first_half, zero, qp)], axis=0)
        s_scr[j] = lax.dot_general(jnp.concatenate([q2, eye2], axis=1),
                                   jnp.concatenate([kp, bias_t], axis=1), nt,
                                   preferred_element_type=F32)
    mxs = []
    for j in range(n_pair):
        sc = s_scr[j]
        mx = jnp.max(sc, axis=-1, keepdims=True)
        p_scr[j] = jnp.exp(sc - mx).astype(BF16)
        mxs.append(mx)
    m_tile = jnp.zeros((A_QBLK, LANES), F32)
    l_tile = jnp.ones((A_QBLK, LANES), F32)
    for j in range(n_pair):
        ps = slice(j * LANES, (j + 1) * LANES)
        vp = v_ref[pl.ds(k_row, A_KWIN), ps]
        ov = jnp.dot(p_scr[j], jnp.concatenate([vp, ones], axis=1), preferred_element_type=F32)
        o_sel = jnp.where(first_half, ov[:A_QBLK, :LANES], ov[A_QBLK:, :LANES])
        l_sel = jnp.where(first_half, ov[:A_QBLK, LANES:], ov[A_QBLK:, LANES:])
        o_ref[pl.ds(out_row, A_QBLK), ps] = (o_sel / l_sel).astype(BF16)
        for e in range(2):
            rs = slice(e * A_QBLK, (e + 1) * A_QBLK)
            m_tile = jnp.where(head_of_lane == 2 * j + e, mxs[j][rs], m_tile)
            l_tile = jnp.where(head_of_lane == 2 * j + e, ov[rs, LANES:], l_tile)
    lse_ref[pl.ds(out_row, A_QBLK), :] = m_tile + jnp.log(l_tile)


def _window_start(q0, cls_len):
    return jnp.clip(q0 - A_HALF, 0, cls_len - A_KWIN)


def _attn_a_natural_kernel(q_ref, k_ref, v_ref, bias_ref, eye_ref, o_ref, lse_ref, s_scr, p_scr,
                           *, tq, cls_len):
    t = pl.program_id(1)
    cst = _band_consts()

    def body(qi, carry):
        q0 = t * tq + qi * A_QBLK
        k0 = _window_start(q0, cls_len)
        _band_block(cst, q_ref, k_ref, v_ref, bias_ref, eye_ref, o_ref, lse_ref, s_scr, p_scr,
                    qi * A_QBLK, k0, qi * A_QBLK, (q0 - k0) // A_HALF)
        return carry

    lax.fori_loop(0, tq // A_QBLK, body, 0, unroll=2)


def _attn_a_classes_kernel(q_ref, k_ref, v_ref, bias_ref, eye_ref, o_ref, lse_ref,
                           qs, ks, vs, os_, ls_, s_scr, p_scr, *, n_tiles, n_cls, c, cls_len):
    for t in range(n_tiles):
        for cl in range(n_cls):
            rows = slice(cl * cls_len + t * c, cl * cls_len + (t + 1) * c)
            qs[rows, :] = q_ref[t, cl]
            ks[rows, :] = k_ref[t, cl]
            vs[rows, :] = v_ref[t, cl]
    cst = _band_consts()
    nq = cls_len // A_QBLK

    def body(blk, carry):
        base = (blk // nq) * cls_len
        q0 = (blk % nq) * A_QBLK
        k0 = _window_start(q0, cls_len)
        _band_block(cst, qs, ks, vs, bias_ref, eye_ref, os_, ls_, s_scr, p_scr,
                    base + q0, base + k0, base + q0, (q0 - k0) // A_HALF)
        return carry

    lax.fori_loop(0, n_cls * nq, body, 0, unroll=2)
    for t in range(n_tiles):
        for cl in range(n_cls):
            rows = slice(cl * cls_len + t * c, cl * cls_len + (t + 1) * c)
            o_ref[t, cl] = os_[rows, :]
            lse_ref[t, cl] = ls_[rows, :]


def _attn_a(qkv_a, band_bias, eye2, b, s, gi, dil):
    cls_len = s // dil
    assert cls_len >= A_KWIN and cls_len % A_QBLK == 0 and (cls_len // A_QBLK) % 2 == 0
    stage = [pltpu.VMEM((A_HEADS // 2, 2 * A_QBLK, A_KWIN), F32),
             pltpu.VMEM((A_HEADS // 2, 2 * A_QBLK, A_KWIN), BF16)]
    const_specs = [pl.BlockSpec(band_bias.shape, lambda *_: (0, 0, 0)),
                   pl.BlockSpec(eye2.shape, lambda *_: (0, 0))]
    if dil == 1:
        tq = min(cls_len, 1024)
        view = qkv_a.reshape(b, s, A_QKV)
        kern = functools.partial(_attn_a_natural_kernel, tq=tq, cls_len=cls_len)
        o, lse = pl.pallas_call(
            kern,
            grid=(b, cls_len // tq),
            in_specs=[
                pl.BlockSpec((None, tq, BRANCH_WIDTH), lambda bi, t: (bi, t, 3 * gi)),
                pl.BlockSpec((None, cls_len, BRANCH_WIDTH), lambda bi, t: (bi, 0, 3 * gi + 1)),
                pl.BlockSpec((None, cls_len, BRANCH_WIDTH), lambda bi, t: (bi, 0, 3 * gi + 2)),
            ] + const_specs,
            out_specs=[
                pl.BlockSpec((None, tq, BRANCH_WIDTH), lambda bi, t: (bi, t, 0)),
                pl.BlockSpec((None, tq, LANES), lambda bi, t: (bi, t, 0)),
            ],
            out_shape=[jax.ShapeDtypeStruct((b, s, BRANCH_WIDTH), BF16),
                       jax.ShapeDtypeStruct((b, s, LANES), F32)],
            scratch_shapes=stage,
            compiler_params=_cparams(("parallel", "arbitrary")),
            name=f"attn_a{gi}",
        )(view, view, view, band_bias, eye2)
    else:
        n_tiles = s // PERM_TILE
        c = PERM_TILE // dil
        n_cls = max(1, 1024 // cls_len)
        assert dil % n_cls == 0 and c % 16 == 0
        view = qkv_a.reshape(b, n_tiles, dil, c, A_QKV)
        kern = functools.partial(_attn_a_classes_kernel, n_tiles=n_tiles, n_cls=n_cls, c=c, cls_len=cls_len)
        rows = n_cls * cls_len
        blk = lambda width: (None, n_tiles, n_cls, c, width)
        o, lse = pl.pallas_call(
            kern,
            grid=(b, dil // n_cls),
            in_specs=[
                pl.BlockSpec(blk(BRANCH_WIDTH), lambda bi, r: (bi, 0, r, 0, 3 * gi)),
                pl.BlockSpec(blk(BRANCH_WIDTH), lambda bi, r: (bi, 0, r, 0, 3 * gi + 1)),
                pl.BlockSpec(blk(BRANCH_WIDTH), lambda bi, r: (bi, 0, r, 0, 3 * gi + 2)),
            ] + const_specs,
            out_specs=[
                pl.BlockSpec(blk(BRANCH_WIDTH), lambda bi, r: (bi, 0, r, 0, 0)),
                pl.BlockSpec(blk(LANES), lambda bi, r: (bi, 0, r, 0, 0)),
            ],
            out_shape=[jax.ShapeDtypeStruct((b, n_tiles, dil, c, BRANCH_WIDTH), BF16),
                       jax.ShapeDtypeStruct((b, n_tiles, dil, c, LANES), F32)],
            scratch_shapes=[pltpu.VMEM((rows, BRANCH_WIDTH), BF16)] * 4
                           + [pltpu.VMEM((rows, LANES), F32)] + stage,
            compiler_params=_cparams(("parallel", "arbitrary")),
            name=f"attn_a{gi}",
        )(view, view, view, band_bias, eye2)
    return o.reshape(b * s, BRANCH_WIDTH), lse.reshape(b * s, LANES)


def _attn_b_kernel(q_ref, k_ref, v_ref, o_ref, kdup, vext, *, tq):
    s = k_ref.shape[0]

    @pl.when(pl.program_id(1) == 0)
    def _():
        k = k_ref[...]
        v = v_ref[...]
        ones = jnp.ones((s, LANES), BF16)
        for kh in range(B_KV_HEADS):
            ks = slice(kh * HEAD_DIM, (kh + 1) * HEAD_DIM)
            kdup[kh] = jnp.concatenate([k[:, ks], k[:, ks]], axis=1)
            vext[kh] = jnp.concatenate([v[:, ks], v[:, ks], ones], axis=1)

    sub = B_ROWS
    first_half = lax.broadcasted_iota(jnp.int32, (sub, LANES), 1) < HEAD_DIM
    pairs_per_kv = B_HEADS // B_KV_HEADS // 2
    nt = (((1,), (1,)), ((), ()))
    for hh in range(tq // sub):
        rows = slice(hh * sub, (hh + 1) * sub)
        for j in range(B_HEADS // 2):
            kh = j // pairs_per_kv
            ps = slice(j * LANES, (j + 1) * LANES)
            qp = q_ref[rows, ps]
            zero = jnp.zeros_like(qp)
            q2 = jnp.concatenate([jnp.where(first_half, qp, zero), jnp.where(first_half, zero, qp)], axis=0)
            sc = lax.dot_general(q2, kdup[kh], nt, preferred_element_type=F32)
            mx = jnp.max(sc, axis=-1, keepdims=True)
            p = jnp.exp(sc - mx).astype(BF16)
            ov = jnp.dot(p, vext[kh], preferred_element_type=F32)
            o_sel = jnp.where(first_half, ov[:sub, :LANES], ov[sub:, :LANES])
            l_sel = jnp.where(first_half, ov[:sub, LANES:], ov[sub:, LANES:])
            o_ref[rows, ps] = (o_sel / l_sel).astype(BF16)


def _attn_b(qkv_b, b, s, tq):
    view = qkv_b.reshape(b, s, B_QKV)
    o = pl.pallas_call(
        functools.partial(_attn_b_kernel, tq=tq),
        grid=(b, s // tq),
        in_specs=[
            pl.BlockSpec((None, tq, B_Q), lambda bi, t: (bi, t, 0)),
            pl.BlockSpec((None, s, B_KV), lambda bi, t: (bi, 0, B_Q // B_KV)),
            pl.BlockSpec((None, s, B_KV), lambda bi, t: (bi, 0, B_Q // B_KV + 1)),
        ],
        out_specs=pl.BlockSpec((None, tq, B_Q), lambda bi, t: (bi, t, 0)),
        out_shape=jax.ShapeDtypeStruct((b, s, B_Q), BF16),
        scratch_shapes=[pltpu.VMEM((B_KV_HEADS, s, LANES), BF16),
                        pltpu.VMEM((B_KV_HEADS, s, 2 * LANES), BF16)],
        compiler_params=_cparams(("parallel", "arbitrary")),
        name="attn_b",
    )(view, view, view)
    return o.reshape(b * s, B_Q)


def _attn_m_kernel(q_ref, k_ref, v_ref, o_ref):
    scale = M_HEAD_DIM ** -0.5
    outs = []
    for h in range(M_HEADS):
        hs = slice(h * M_HEAD_DIM, (h + 1) * M_HEAD_DIM)
        sc = lax.dot_general(q_ref[:, hs], k_ref[:, hs], (((1,), (1,)), ((), ())),
                             preferred_element_type=F32) * scale
        mx = jnp.max(sc, axis=-1, keepdims=True)
        p = jnp.exp(sc - mx)
        l = jnp.sum(p, axis=-1, keepdims=True)
        o = jnp.dot(p.astype(BF16), v_ref[:, hs], preferred_element_type=F32)
        outs.append(o / l)
    o_ref[...] = jnp.concatenate(outs, axis=1).astype(BF16)


def _attn_m(mq, kvm, b, s, n_mem, tq):
    qv = mq.reshape(b, s, M_Q)
    kv = kvm.reshape(b, n_mem, 2 * M_Q)
    o = pl.pallas_call(
        _attn_m_kernel,
        grid=(b, s // tq),
        in_specs=[
            pl.BlockSpec((None, tq, M_Q), lambda bi, t: (bi, t, 0)),
            pl.BlockSpec((None, n_mem, M_Q), lambda bi, t: (bi, 0, 0)),
            pl.BlockSpec((None, n_mem, M_Q), lambda bi, t: (bi, 0, 1)),
        ],
        out_specs=pl.BlockSpec((None, tq, M_Q), lambda bi, t: (bi, t, 0)),
        out_shape=jax.ShapeDtypeStruct((b, s, M_Q), BF16),
        compiler_params=_cparams(("parallel", "arbitrary")),
        name="attn_m",
    )(qv, kv, kv)
    return o.reshape(b * s, M_Q)


def _final_kernel(x_ref, oa0_ref, oa1_ref, oa2_ref, l0_ref, l1_ref, l2_ref, ob_ref, om_ref,
                  gpre_ref, wg_ref, bm_ref, wa_ref, wb_ref, wm_ref, wout_ref, gpost_ref,
                  exp_ref, o_ref, nat_o, nat_l, *, tm, dils):
    n_ch = BRANCH_WIDTH // LANES
    group_refs = list(zip(dils, (oa0_ref, oa1_ref, oa2_ref), (l0_ref, l1_ref, l2_ref)))

    for gi, (dil, og_ref, l_ref) in enumerate(group_refs):
        if dil == 1:
            continue
        cc = tm // dil
        for r in range(dil):
            o = og_ref[r].astype(F32)
            for ch in range(n_ch):
                nat_o[gi * n_ch + ch, pl.ds(r, cc, stride=dil), :] = o[:, ch * LANES:(ch + 1) * LANES]
            nat_l[gi, pl.ds(r, cc, stride=dil), :] = l_ref[r]

    expand = exp_ref[...]
    for sb in range(tm // FINAL_ROWS):
        rows = slice(sb * FINAL_ROWS, (sb + 1) * FINAL_ROWS)
        xf = x_ref[rows, :]
        h = _rms_norm_rows(xf, gpre_ref[...]).astype(BF16)
        og, lses = [], []
        for gi, (dil, og_ref, l_ref) in enumerate(group_refs):
            if dil == 1:
                o = og_ref[rows, :].astype(F32)
                og.append([o[:, ch * LANES:(ch + 1) * LANES] for ch in range(n_ch)])
                lses.append(l_ref[rows, :])
            else:
                og.append([nat_o[gi * n_ch + ch, rows, :] for ch in range(n_ch)])
                lses.append(nat_l[gi, rows, :])

        mx = functools.reduce(jnp.maximum, lses)
        es = [jnp.exp(l - mx) for l in lses]
        inv_z = 1.0 / functools.reduce(lambda a, c: a + c, es)
        oa_ch = [None] * n_ch
        for e, o_chunks in zip(es, og):
            w = e * inv_z
            hi = w.astype(BF16)
            lo = (w - hi.astype(F32)).astype(BF16)
            w_wide = (jnp.dot(hi, expand, preferred_element_type=F32)
                      + jnp.dot(lo, expand, preferred_element_type=F32))
            for ch in range(n_ch):
                term = w_wide[:, ch * LANES:(ch + 1) * LANES] * o_chunks[ch]
                oa_ch[ch] = term if oa_ch[ch] is None else oa_ch[ch] + term
        oa = jnp.concatenate(oa_ch, axis=1)

        def branch(o, idx, w_ref):
            gs = slice(idx * BRANCH_WIDTH, (idx + 1) * BRANCH_WIDTH)
            ms = slice(3 * BRANCH_WIDTH + idx * D_MODEL, 3 * BRANCH_WIDTH + (idx + 1) * D_MODEL)
            ms_b = slice(idx * D_MODEL, (idx + 1) * D_MODEL)
            gate = jnp.dot(h, wg_ref[:, gs], preferred_element_type=F32)
            gate = gate * jax.nn.sigmoid(gate)
            y = jnp.dot((o * gate).astype(BF16), w_ref[...], preferred_element_type=F32)
            mg = jnp.dot(h, wg_ref[:, ms], preferred_element_type=F32) + bm_ref[:, ms_b]
            return jax.nn.sigmoid(mg) * y

        merged = branch(oa, 0, wa_ref)
        merged = merged + branch(ob_ref[rows, :].astype(F32), 1, wb_ref)
        merged = merged + branch(om_ref[rows, :].astype(F32), 2, wm_ref)
        out = jnp.dot(merged.astype(BF16), wout_ref[...], preferred_element_type=F32)
        o_ref[rows, :] = xf + _rms_norm_rows(out, gpost_ref[...])


def _final(x2, oas, lses, ob, om, g_pre, w_g, b_merge, w_a, w_b, w_m, w_out, g_post, expand, dils, tm):
    m = x2.shape[0]
    per_tile = PERM_TILE // tm
    row = lambda n: pl.BlockSpec((tm, n), lambda i: (i, 0))
    full = lambda a: pl.BlockSpec(a.shape, lambda i: (0, 0), pipeline_mode=pl.Buffered(1))

    def group_operand(a, dil, width):
        if dil == 1:
            return a, row(width)
        cc = tm // dil
        view = a.reshape(m // PERM_TILE, dil, per_tile, cc, width)
        return view, pl.BlockSpec((None, dil, None, cc, width),
                                  lambda i: (i // per_tile, 0, i % per_tile, 0, 0))

    o_ops = [group_operand(a, dil, BRANCH_WIDTH) for a, dil in zip(oas, dils)]
    l_ops = [group_operand(a, dil, LANES) for a, dil in zip(lses, dils)]
    consts = (g_pre, w_g, b_merge, w_a, w_b, w_m, w_out, g_post, expand)
    kern = functools.partial(_final_kernel, tm=tm, dils=dils)
    n_ch = BRANCH_WIDTH // LANES
    return pl.pallas_call(
        kern,
        grid=(m // tm,),
        in_specs=([row(D_MODEL)] + [spec for _, spec in o_ops] + [spec for _, spec in l_ops]
                  + [row(BRANCH_WIDTH)] * 2 + [full(a) for a in consts]),
        out_specs=row(D_MODEL),
        out_shape=jax.ShapeDtypeStruct((m, D_MODEL), F32),
        scratch_shapes=[pltpu.VMEM((len(dils) * n_ch, tm, LANES), F32),
                        pltpu.VMEM((len(dils), tm, LANES), F32)],
        compiler_params=_cparams(("parallel",)),
        name="final",
    )(x2, *[a for a, _ in o_ops], *[a for a, _ in l_ops], ob, om, *consts)


def _rotary_tables(pos_per_lane, inv_per_lane, is_second_half):
    ang = pos_per_lane * inv_per_lane[None, :]
    cos, sin = jnp.cos(ang), jnp.sin(ang)
    zero = jnp.zeros_like(sin)
    return jnp.stack([cos,
                      jnp.where(is_second_half[None, :], sin, zero),
                      jnp.where(is_second_half[None, :], zero, -sin)]).astype(F32)


def _class_major_positions(s, dil):
    n = np.arange(s)
    within = n % PERM_TILE
    c = PERM_TILE // dil
    return (n // PERM_TILE) * PERM_TILE + (within % c) * dil + within // c


def _tables_a(s, dils):
    half = HEAD_DIM // 2
    lane = np.arange(LANES)
    inv = jnp.power(ROPE_THETA, -jnp.asarray(lane % half, F32) * 2.0 / HEAD_DIM)
    second = jnp.asarray((lane % HEAD_DIM) >= half)
    tables = []
    for dil in dils:
        pos = jnp.asarray(_class_major_positions(s, dil), F32)
        tables.append(_rotary_tables(jnp.broadcast_to(pos[:, None], (s, LANES)), inv, second))
    return jnp.stack(tables)


def _band_tables():
    q = np.arange(A_QBLK)[None, :, None]
    k = np.arange(A_KWIN)[None, None, :]
    delta = (np.arange(3) * A_HALF)[:, None, None]
    bias = np.where(np.abs(q - k + delta) <= A_HALF, 0.0, NEG_INF).astype(np.float32)
    eye2 = np.concatenate([np.eye(A_QBLK), np.eye(A_QBLK)], axis=0)
    return jnp.asarray(bias.transpose(0, 2, 1), BF16), jnp.asarray(eye2, BF16)


def _tables_b(s):
    dr = HEAD_DIM // 2
    half = dr // 2
    lane = np.arange(LANES)
    inv = jnp.power(ROPE_THETA, -jnp.asarray(lane % half, F32) * 2.0 / dr)
    row_pos = jnp.repeat(jnp.arange(s // GRID_W, dtype=F32), GRID_W)
    col_pos = (jnp.arange(s) % GRID_W).astype(F32)
    uses_col = jnp.asarray((lane % HEAD_DIM) >= dr)
    pos = jnp.where(uses_col[None, :], col_pos[:, None], row_pos[:, None])
    return _rotary_tables(pos, inv, jnp.asarray((lane % dr) >= half))


def kernel(x, mem, g_pre, w_in, b_merge, q_norm, k_norm, g_mem, w_mem_kv, w_br_a, w_br_b, w_br_m, w_out, g_post):
    b, s, d = x.shape
    n_mem = mem.shape[1]
    depth = g_pre.shape[0]
    n_groups = len(A_GROUPS)

    dils = tuple(dil for _, dil in A_GROUPS)
    assert s % PERM_TILE == 0
    t_a = _tables_a(s, dils)
    band_bias, eye2 = _band_tables()
    t_b = _tables_b(s)
    lane = np.arange(LANES)
    ones_bd = jnp.asarray((lane[:, None] // HEAD_DIM) == (lane[None, :] // HEAD_DIM), BF16)
    wide = np.arange(BRANCH_WIDTH)
    expand = jnp.asarray(lane[:, None] == (wide[None, :] // HEAD_DIM) * LSE_LANES_PER_HEAD, BF16)

    for layer in range(depth):
        w = w_in[layer]
        w_a = w[:, :A_QKV].reshape(d, 3, n_groups, BRANCH_WIDTH)
        w_a = w_a * jnp.asarray([HEAD_DIM ** -0.5, 1.0, 1.0], F32)[None, :, None, None]
        w_a = w_a.transpose(0, 2, 1, 3).reshape(d, A_QKV).astype(BF16)
        off = A_QKV
        w_bm = w[:, off:off + B_QKV + M_Q].astype(BF16)
        off += B_QKV + M_Q
        w_g = w[:, off:].astype(BF16)
        gq = jnp.tile(q_norm[layer] * HEAD_DIM ** -0.5, LANES // HEAD_DIM)[None, :]
        gk = jnp.tile(k_norm[layer], LANES // HEAD_DIM)[None, :]

        x2 = x.reshape(b * s, d)
        gp = g_pre[layer][None, :]
        qkv_a = _proj_a(x2, gp, w_a, t_a, s, dils)
        qkv_b, mq = _proj_bm(x2, gp, w_bm, t_b, gq, gk, ones_bd, s, tm=1024)
        kvm = _mem_kv(mem.reshape(b * n_mem, d), g_mem[layer][None, :],
                      w_mem_kv[layer].astype(BF16), tm=n_mem)

        oas, lses = [], []
        for gi, (_, dil) in enumerate(A_GROUPS):
            o_g, lse_g = _attn_a(qkv_a, band_bias, eye2, b, s, gi, dil)
            oas.append(o_g)
            lses.append(lse_g)
        ob = _attn_b(qkv_b, b, s, tq=512)
        om = _attn_m(mq, kvm, b, s, n_mem, tq=1024)

        x2 = _final(x2, oas, lses, ob, om, gp, w_g, b_merge[layer][None, :],
                    w_br_a[layer].astype(BF16), w_br_b[layer].astype(BF16),
                    w_br_m[layer].astype(BF16), w_out[layer].astype(BF16),
                    g_post[layer][None, :], expand, dils, tm=512)
        x = x2.reshape(b, s, d)
    return x
```

```python
import functools

import jax
import jax.numpy as jnp
import numpy as np
from jax import lax
from jax.experimental import pallas as pl
from jax.experimental.pallas import tpu as pltpu

D_MODEL = 1024
HEAD_DIM = 64
BRANCH_WIDTH = 512
A_GROUPS = ((128, 1), (512, 4), (2048, 16))
A_HEADS = 8
B_HEADS = 8
B_KV_HEADS = 2
M_HEADS = 4
M_HEAD_DIM = 128
GRID_W = 64
ROPE_THETA = 10000.0
NORM_EPS = 1e-6
NEG_INF = -1e30
N_BRANCHES = 3

A_QKV = 3 * len(A_GROUPS) * A_HEADS * HEAD_DIM
B_Q = B_HEADS * HEAD_DIM
B_KV = B_KV_HEADS * HEAD_DIM
B_QKV = B_Q + 2 * B_KV
M_Q = M_HEADS * M_HEAD_DIM
GATE_W = 3 * BRANCH_WIDTH + N_BRANCHES * D_MODEL

LANES = 128
A_QBLK = 128
A_KWIN = 256
A_HALF = 64
LSE_LANES_PER_HEAD = LANES // A_HEADS
PERM_TILE = 1024
ROW_CHUNK = 256
B_ROWS = 128
FINAL_ROWS = 256
VMEM_LIMIT = 56 * 1024 * 1024

F32 = jnp.float32
BF16 = jnp.bfloat16


def _cparams(sem):
    return pltpu.CompilerParams(dimension_semantics=sem, vmem_limit_bytes=VMEM_LIMIT)


def _rms_norm_rows(xf, g):
    return xf * lax.rsqrt(jnp.mean(xf * xf, axis=-1, keepdims=True) + NORM_EPS) * g


def _rotate(x, c, s_fwd, s_bwd, shift):
    return (x * c + pltpu.roll(x, shift, axis=1) * s_fwd
            + pltpu.roll(x, LANES - shift, axis=1) * s_bwd)


def _proj_a_kernel(x_ref, g_ref, w_ref, t_ref, o_ref, hf_ref, hp_ref, *, tm, dils):
    j = pl.program_id(1)
    n_lane = D_MODEL // LANES

    n_rot = 2 * BRANCH_WIDTH // LANES

    def prepare(gi):
        dil = dils[gi]
        c = tm // dil
        for r in range(dil):
            for ch in range(n_lane):
                hp_ref[gi, r * c:(r + 1) * c, ch * LANES:(ch + 1) * LANES] = (
                    hf_ref[ch, pl.ds(r, c, stride=dil), :].astype(BF16))

    def project(gi):
        for rc in range(tm // ROW_CHUNK):
            rows = slice(rc * ROW_CHUNK, (rc + 1) * ROW_CHUNK)
            acc = jnp.dot(hp_ref[gi, rows, :], w_ref[...], preferred_element_type=F32)
            c, sf, sb = t_ref[0, rows, :], t_ref[1, rows, :], t_ref[2, rows, :]
            for ch in range(n_rot):
                sl = slice(ch * LANES, (ch + 1) * LANES)
                o_ref[rows, sl] = _rotate(acc[:, sl], c, sf, sb, HEAD_DIM // 2).astype(BF16)
            o_ref[rows, n_rot * LANES:] = acc[:, n_rot * LANES:].astype(BF16)

    for gi in range(len(dils)):
        @pl.when(j == gi)
        def _(gi=gi):
            if gi == 0:
                h = _rms_norm_rows(x_ref[...], g_ref[...])
                for ch in range(n_lane):
                    hf_ref[ch] = h[:, ch * LANES:(ch + 1) * LANES]
                prepare(0)
            if gi + 1 < len(dils):
                prepare(gi + 1)
            project(gi)


def _proj_a(x2, g_pre, w_a, t_a, s, dils):
    m = x2.shape[0]
    tm = PERM_TILE
    n_sblk = s // tm
    group_w = 3 * BRANCH_WIDTH
    kern = functools.partial(_proj_a_kernel, tm=tm, dils=dils)
    return pl.pallas_call(
        kern,
        grid=(m // tm, len(dils)),
        in_specs=[
            pl.BlockSpec((tm, D_MODEL), lambda i, j: (i, 0)),
            pl.BlockSpec((1, D_MODEL), lambda i, j: (0, 0)),
            pl.BlockSpec((D_MODEL, group_w), lambda i, j: (0, j)),
            pl.BlockSpec((None, 3, tm, LANES), lambda i, j: (j, 0, i % n_sblk, 0)),
        ],
        out_specs=pl.BlockSpec((tm, group_w), lambda i, j: (i, j)),
        out_shape=jax.ShapeDtypeStruct((m, A_QKV), BF16),
        scratch_shapes=[pltpu.VMEM((D_MODEL // LANES, tm, LANES), F32),
                        pltpu.VMEM((len(dils), tm, D_MODEL), BF16)],
        compiler_params=_cparams(("parallel", "arbitrary")),
        name="proj_a",
    )(x2, g_pre, w_a, t_a)


def _proj_bm_kernel(x_ref, g_ref, w_ref, t_ref, gq_ref, gk_ref, ones_ref, ob_ref, om_ref):
    ones_bd = ones_ref[...]
    n_qk = (B_Q + B_KV) // LANES
    for rc in range(x_ref.shape[0] // ROW_CHUNK):
        rows = slice(rc * ROW_CHUNK, (rc + 1) * ROW_CHUNK)
        h = _rms_norm_rows(x_ref[rows, :], g_ref[...]).astype(BF16)
        acc = jnp.dot(h, w_ref[...], preferred_element_type=F32)
        c, sf, sb = t_ref[0, rows, :], t_ref[1, rows, :], t_ref[2, rows, :]
        for ch in range(n_qk):
            sl = slice(ch * LANES, (ch + 1) * LANES)
            y = acc[:, sl]
            sq = y * y
            hi = sq.astype(BF16)
            lo = (sq - hi.astype(F32)).astype(BF16)
            ss = (jnp.dot(hi, ones_bd, preferred_element_type=F32)
                  + jnp.dot(lo, ones_bd, preferred_element_type=F32))
            gain = gq_ref[...] if ch < B_Q // LANES else gk_ref[...]
            y = y * lax.rsqrt(ss * (1.0 / HEAD_DIM) + NORM_EPS) * gain
            ob_ref[rows, sl] = _rotate(y, c, sf, sb, HEAD_DIM // 4).astype(BF16)
        ob_ref[rows, B_Q + B_KV:] = acc[:, B_Q + B_KV:B_QKV].astype(BF16)
        om_ref[rows, :] = acc[:, B_QKV:].astype(BF16)


def _proj_bm(x2, g_pre, w_bm, t_b, gq, gk, ones_bd, s, tm):
    m = x2.shape[0]
    n_sblk = s // tm
    n = B_QKV + M_Q
    return pl.pallas_call(
        _proj_bm_kernel,
        grid=(m // tm,),
        in_specs=[
            pl.BlockSpec((tm, D_MODEL), lambda i: (i, 0)),
            pl.BlockSpec((1, D_MODEL), lambda i: (0, 0)),
            pl.BlockSpec((D_MODEL, n), lambda i: (0, 0)),
            pl.BlockSpec((3, tm, LANES), lambda i: (0, i % n_sblk, 0)),
            pl.BlockSpec((1, LANES), lambda i: (0, 0)),
            pl.BlockSpec((1, LANES), lambda i: (0, 0)),
            pl.BlockSpec((LANES, LANES), lambda i: (0, 0)),
        ],
        out_specs=[
            pl.BlockSpec((tm, B_QKV), lambda i: (i, 0)),
            pl.BlockSpec((tm, M_Q), lambda i: (i, 0)),
        ],
        out_shape=[jax.ShapeDtypeStruct((m, B_QKV), BF16),
                   jax.ShapeDtypeStruct((m, M_Q), BF16)],
        compiler_params=_cparams(("parallel",)),
        name="proj_bm",
    )(x2, g_pre, w_bm, t_b, gq, gk, ones_bd)


def _mem_kv_kernel(x_ref, g_ref, w_ref, o_ref):
    h = _rms_norm_rows(x_ref[...], g_ref[...]).astype(BF16)
    o_ref[...] = jnp.dot(h, w_ref[...], preferred_element_type=F32).astype(BF16)


def _mem_kv(mem2, g_mem, w_kv, tm):
    m = mem2.shape[0]
    n = w_kv.shape[1]
    return pl.pallas_call(
        _mem_kv_kernel,
        grid=(m // tm,),
        in_specs=[
            pl.BlockSpec((tm, D_MODEL), lambda i: (i, 0)),
            pl.BlockSpec((1, D_MODEL), lambda i: (0, 0)),
            pl.BlockSpec((D_MODEL, n), lambda i: (0, 0)),
        ],
        out_specs=pl.BlockSpec((tm, n), lambda i: (i, 0)),
        out_shape=jax.ShapeDtypeStruct((m, n), BF16),
        compiler_params=_cparams(("parallel",)),
        name="mem_kv",
    )(mem2, g_mem, w_kv)


def _band_consts():
    lane = lax.broadcasted_iota(jnp.int32, (A_QBLK, LANES), 1)
    return dict(first_half=lane < HEAD_DIM, head_of_lane=lane // LSE_LANES_PER_HEAD,
                ones=jnp.ones((A_KWIN, LANES), BF16))


def _band_block(cst, q_ref, k_ref, v_ref, bias_ref, eye_ref, o_ref, lse_ref, s_scr, p_scr,
                q_row, k_row, out_row, band_type):
    first_half, head_of_lane, ones = cst["first_half"], cst["head_of_lane"], cst["ones"]
    q_row = pl.multiple_of(q_row, A_QBLK)
    k_row = pl.multiple_of(k_row, A_HALF)
    out_row = pl.multiple_of(out_row, A_QBLK)
    bias_t = bias_ref[band_type]
    eye2 = eye_ref[...]
    n_pair = A_HEADS // 2
    nt = (((1,), (1,)), ((), ()))
    for j in range(n_pair):
        ps = slice(j * LANES, (j + 1) * LANES)
        qp = q_ref[pl.ds(q_row, A_QBLK), ps]
        kp = k_ref[pl.ds(k_row, A_KWIN), ps]
        zero = jnp.zeros_like(qp)
        q2 = jnp.concatenate([jnp.where(first_half, qp, zero), jnp.where(first_half, zero, qp)], axis=0)
        s_scr[j] = lax.dot_general(jnp.concatenate([q2, eye2], axis=1),
                                   jnp.concatenate([kp, bias_t], axis=1), nt,
                                   preferred_element_type=F32)
    mxs = []
    for j in range(n_pair):
        sc = s_scr[j]
        mx = jnp.max(sc, axis=-1, keepdims=True)
        p_scr[j] = jnp.exp(sc - mx).astype(BF16)
        mxs.append(mx)
    m_tile = jnp.zeros((A_QBLK, LANES), F32)
    l_tile = jnp.ones((A_QBLK, LANES), F32)
    for j in range(n_pair):
        ps = slice(j * LANES, (j + 1) * LANES)
        vp = v_ref[pl.ds(k_row, A_KWIN), ps]
        ov = jnp.dot(p_scr[j], jnp.concatenate([vp, ones], axis=1), preferred_element_type=F32)
        o_sel = jnp.where(first_half, ov[:A_QBLK, :LANES], ov[A_QBLK:, :LANES])
        l_sel = jnp.where(first_half, ov[:A_QBLK, LANES:], ov[A_QBLK:, LANES:])
        o_ref[pl.ds(out_row, A_QBLK), ps] = (o_sel / l_sel).astype(BF16)
        for e in range(2):
            rs = slice(e * A_QBLK, (e + 1) * A_QBLK)
            m_tile = jnp.where(head_of_lane == 2 * j + e, mxs[j][rs], m_tile)
            l_tile = jnp.where(head_of_lane == 2 * j + e, ov[rs, LANES:], l_tile)
    lse_ref[pl.ds(out_row, A_QBLK), :] = m_tile + jnp.log(l_tile)


def _window_start(q0, cls_len):
    return jnp.clip(q0 - A_HALF, 0, cls_len - A_KWIN)


def _attn_a_natural_kernel(q_ref, k_ref, v_ref, bias_ref, eye_ref, o_ref, lse_ref, s_scr, p_scr,
                           *, tq, cls_len):
    t = pl.program_id(1)
    cst = _band_consts()

    def body(qi, carry):
        q0 = t * tq + qi * A_QBLK
        k0 = _window_start(q0, cls_len)
        _band_block(cst, q_ref, k_ref, v_ref, bias_ref, eye_ref, o_ref, lse_ref, s_scr, p_scr,
                    qi * A_QBLK, k0, qi * A_QBLK, (q0 - k0) // A_HALF)
        return carry

    lax.fori_loop(0, tq // A_QBLK, body, 0, unroll=2)


def _attn_a_classes_kernel(q_ref, k_ref, v_ref, bias_ref, eye_ref, o_ref, lse_ref,
                           qs, ks, vs, os_, ls_, s_scr, p_scr, *, n_tiles, n_cls, c, cls_len):
    for t in range(n_tiles):
        for cl in range(n_cls):
            rows = slice(cl * cls_len + t * c, cl * cls_len + (t + 1) * c)
            qs[rows, :] = q_ref[t, cl]
            ks[rows, :] = k_ref[t, cl]
            vs[rows, :] = v_ref[t, cl]
    cst = _band_consts()
    nq = cls_len // A_QBLK

    def body(blk, carry):
        base = (blk // nq) * cls_len
        q0 = (blk % nq) * A_QBLK
        k0 = _window_start(q0, cls_len)
        _band_block(cst, qs, ks, vs, bias_ref, eye_ref, os_, ls_, s_scr, p_scr,
                    base + q0, base + k0, base + q0, (q0 - k0) // A_HALF)
        return carry

    lax.fori_loop(0, n_cls * nq, body, 0, unroll=2)
    for t in range(n_tiles):
        for cl in range(n_cls):
            rows = slice(cl * cls_len + t * c, cl * cls_len + (t + 1) * c)
            o_ref[t, cl] = os_[rows, :]
            lse_ref[t, cl] = ls_[rows, :]


def _attn_a(qkv_a, band_bias, eye2, b, s, gi, dil):
    cls_len = s // dil
    assert cls_len >= A_KWIN and cls_len % A_QBLK == 0 and (cls_len // A_QBLK) % 2 == 0
    stage = [pltpu.VMEM((A_HEADS // 2, 2 * A_QBLK, A_KWIN), F32),
             pltpu.VMEM((A_HEADS // 2, 2 * A_QBLK, A_KWIN), BF16)]
    const_specs = [pl.BlockSpec(band_bias.shape, lambda *_: (0, 0, 0)),
                   pl.BlockSpec(eye2.shape, lambda *_: (0, 0))]
    if dil == 1:
        tq = min(cls_len, 1024)
        view = qkv_a.reshape(b, s, A_QKV)
        kern = functools.partial(_attn_a_natural_kernel, tq=tq, cls_len=cls_len)
        o, lse = pl.pallas_call(
            kern,
            grid=(b, cls_len // tq),
            in_specs=[
                pl.BlockSpec((None, tq, BRANCH_WIDTH), lambda bi, t: (bi, t, 3 * gi)),
                pl.BlockSpec((None, cls_len, BRANCH_WIDTH), lambda bi, t: (bi, 0, 3 * gi + 1)),
                pl.BlockSpec((None, cls_len, BRANCH_WIDTH), lambda bi, t: (bi, 0, 3 * gi + 2)),
            ] + const_specs,
            out_specs=[
                pl.BlockSpec((None, tq, BRANCH_WIDTH), lambda bi, t: (bi, t, 0)),
                pl.BlockSpec((None, tq, LANES), lambda bi, t: (bi, t, 0)),
            ],
            out_shape=[jax.ShapeDtypeStruct((b, s, BRANCH_WIDTH), BF16),
                       jax.ShapeDtypeStruct((b, s, LANES), F32)],
            scratch_shapes=stage,
            compiler_params=_cparams(("parallel", "arbitrary")),
            name=f"attn_a{gi}",
        )(view, view, view, band_bias, eye2)
    else:
        n_tiles = s // PERM_TILE
        c = PERM_TILE // dil
        n_cls = max(1, 1024 // cls_len)
        assert dil % n_cls == 0 and c % 16 == 0
        view = qkv_a.reshape(b, n_tiles, dil, c, A_QKV)
        kern = functools.partial(_attn_a_classes_kernel, n_tiles=n_tiles, n_cls=n_cls, c=c, cls_len=cls_len)
        rows = n_cls * cls_len
        blk = lambda width: (None, n_tiles, n_cls, c, width)
        o, lse = pl.pallas_call(
            kern,
            grid=(b, dil // n_cls),
            in_specs=[
                pl.BlockSpec(blk(BRANCH_WIDTH), lambda bi, r: (bi, 0, r, 0, 3 * gi)),
                pl.BlockSpec(blk(BRANCH_WIDTH), lambda bi, r: (bi, 0, r, 0, 3 * gi + 1)),
                pl.BlockSpec(blk(BRANCH_WIDTH), lambda bi, r: (bi, 0, r, 0, 3 * gi + 2)),
            ] + const_specs,
            out_specs=[
                pl.BlockSpec(blk(BRANCH_WIDTH), lambda bi, r: (bi, 0, r, 0, 0)),
                pl.BlockSpec(blk(LANES), lambda bi, r: (bi, 0, r, 0, 0)),
            ],
            out_shape=[jax.ShapeDtypeStruct((b, n_tiles, dil, c, BRANCH_WIDTH), BF16),
                       jax.ShapeDtypeStruct((b, n_tiles, dil, c, LANES), F32)],
            scratch_shapes=[pltpu.VMEM((rows, BRANCH_WIDTH), BF16)] * 4
                           + [pltpu.VMEM((rows, LANES), F32)] + stage,
            compiler_params=_cparams(("parallel", "arbitrary")),
            name=f"attn_a{gi}",
        )(view, view, view, band_bias, eye2)
    return o.reshape(b * s, BRANCH_WIDTH), lse.reshape(b * s, LANES)


def _attn_b_kernel(q_ref, k_ref, v_ref, o_ref, kdup, vext, *, tq):
    s = k_ref.shape[0]

    @pl.when(pl.program_id(1) == 0)
    def _():
        k = k_ref[...]
        v = v_ref[...]
        ones = jnp.ones((s, LANES), BF16)
        for kh in range(B_KV_HEADS):
            ks = slice(kh * HEAD_DIM, (kh + 1) * HEAD_DIM)
            kdup[kh] = jnp.concatenate([k[:, ks], k[:, ks]], axis=1)
            vext[kh] = jnp.concatenate([v[:, ks], v[:, ks], ones], axis=1)

    sub = B_ROWS
    first_half = lax.broadcasted_iota(jnp.int32, (sub, LANES), 1) < HEAD_DIM
    pairs_per_kv = B_HEADS // B_KV_HEADS // 2
    nt = (((1,), (1,)), ((), ()))
    for hh in range(tq // sub):
        rows = slice(hh * sub, (hh + 1) * sub)
        for j in range(B_HEADS // 2):
            kh = j // pairs_per_kv
            ps = slice(j * LANES, (j + 1) * LANES)
            qp = q_ref[rows, ps]
            zero = jnp.zeros_like(qp)
            q2 = jnp.concatenate([jnp.where(first_half, qp, zero), jnp.where(first_half, zero, qp)], axis=0)
            sc = lax.dot_general(q2, kdup[kh], nt, preferred_element_type=F32)
            mx = jnp.max(sc, axis=-1, keepdims=True)
            p = jnp.exp(sc - mx).astype(BF16)
            ov = jnp.dot(p, vext[kh], preferred_element_type=F32)
            o_sel = jnp.where(first_half, ov[:sub, :LANES], ov[sub:, :LANES])
            l_sel = jnp.where(first_half, ov[:sub, LANES:], ov[sub:, LANES:])
            o_ref[rows, ps] = (o_sel / l_sel).astype(BF16)


def _attn_b(qkv_b, b, s, tq):
    view = qkv_b.reshape(b, s, B_QKV)
    o = pl.pallas_call(
        functools.partial(_attn_b_kernel, tq=tq),
        grid=(b, s // tq),
        in_specs=[
            pl.BlockSpec((None, tq, B_Q), lambda bi, t: (bi, t, 0)),
            pl.BlockSpec((None, s, B_KV), lambda bi, t: (bi, 0, B_Q // B_KV)),
            pl.BlockSpec((None, s, B_KV), lambda bi, t: (bi, 0, B_Q // B_KV + 1)),
        ],
        out_specs=pl.BlockSpec((None, tq, B_Q), lambda bi, t: (bi, t, 0)),
        out_shape=jax.ShapeDtypeStruct((b, s, B_Q), BF16),
        scratch_shapes=[pltpu.VMEM((B_KV_HEADS, s, LANES), BF16),
                        pltpu.VMEM((B_KV_HEADS, s, 2 * LANES), BF16)],
        compiler_params=_cparams(("parallel", "arbitrary")),
        name="attn_b",
    )(view, view, view)
    return o.reshape(b * s, B_Q)


def _attn_m_kernel(q_ref, k_ref, v_ref, o_ref):
    scale = M_HEAD_DIM ** -0.5
    outs = []
    for h in range(M_HEADS):
        hs = slice(h * M_HEAD_DIM, (h + 1) * M_HEAD_DIM)
        sc = lax.dot_general(q_ref[:, hs], k_ref[:, hs], (((1,), (1,)), ((), ())),
                             preferred_element_type=F32) * scale
        mx = jnp.max(sc, axis=-1, keepdims=True)
        p = jnp.exp(sc - mx)
        l = jnp.sum(p, axis=-1, keepdims=True)
        o = jnp.dot(p.astype(BF16), v_ref[:, hs], preferred_element_type=F32)
        outs.append(o / l)
    o_ref[...] = jnp.concatenate(outs, axis=1).astype(BF16)


def _attn_m(mq, kvm, b, s, n_mem, tq):
    qv = mq.reshape(b, s, M_Q)
    kv = kvm.reshape(b, n_mem, 2 * M_Q)
    o = pl.pallas_call(
        _attn_m_kernel,
        grid=(b, s // tq),
        in_specs=[
            pl.BlockSpec((None, tq, M_Q), lambda bi, t: (bi, t, 0)),
            pl.BlockSpec((None, n_mem, M_Q), lambda bi, t: (bi, 0, 0)),
            pl.BlockSpec((None, n_mem, M_Q), lambda bi, t: (bi, 0, 1)),
        ],
        out_specs=pl.BlockSpec((None, tq, M_Q), lambda bi, t: (bi, t, 0)),
        out_shape=jax.ShapeDtypeStruct((b, s, M_Q), BF16),
        compiler_params=_cparams(("parallel", "arbitrary")),
        name="attn_m",
    )(qv, kv, kv)
    return o.reshape(b * s, M_Q)


def _final_kernel(x_ref, oa0_ref, oa1_ref, oa2_ref, l0_ref, l1_ref, l2_ref, ob_ref, om_ref,
                  gpre_ref, wg_ref, bm_ref, wa_ref, wb_ref, wm_ref, wout_ref, gpost_ref,
                  exp_ref, o_ref, nat_o, nat_l, *, tm, dils):
    n_ch = BRANCH_WIDTH // LANES
    group_refs = list(zip(dils, (oa0_ref, oa1_ref, oa2_ref), (l0_ref, l1_ref, l2_ref)))

    for gi, (dil, og_ref, l_ref) in enumerate(group_refs):
        if dil == 1:
            continue
        cc = tm // dil
        for r in range(dil):
            o = og_ref[r].astype(F32)
            for ch in range(n_ch):
                nat_o[gi * n_ch + ch, pl.ds(r, cc, stride=dil), :] = o[:, ch * LANES:(ch + 1) * LANES]
            nat_l[gi, pl.ds(r, cc, stride=dil), :] = l_ref[r]

    expand = exp_ref[...]
    for sb in range(tm // FINAL_ROWS):
        rows = slice(sb * FINAL_ROWS, (sb + 1) * FINAL_ROWS)
        xf = x_ref[rows, :]
        h = _rms_norm_rows(xf, gpre_ref[...]).astype(BF16)
        og, lses = [], []
        for gi, (dil, og_ref, l_ref) in enumerate(group_refs):
            if dil == 1:
                o = og_ref[rows, :].astype(F32)
                og.append([o[:, ch * LANES:(ch + 1) * LANES] for ch in range(n_ch)])
                lses.append(l_ref[rows, :])
            else:
                og.append([nat_o[gi * n_ch + ch, rows, :] for ch in range(n_ch)])
                lses.append(nat_l[gi, rows, :])

        mx = functools.reduce(jnp.maximum, lses)
        es = [jnp.exp(l - mx) for l in lses]
        inv_z = 1.0 / functools.reduce(lambda a, c: a + c, es)
        oa_ch = [None] * n_ch
        for e, o_chunks in zip(es, og):
            w = e * inv_z
            hi = w.astype(BF16)
            lo = (w - hi.astype(F32)).astype(BF16)
            w_wide = (jnp.dot(hi, expand, preferred_element_type=F32)
                      + jnp.dot(lo, expand, preferred_element_type=F32))
            for ch in range(n_ch):
                term = w_wide[:, ch * LANES:(ch + 1) * LANES] * o_chunks[ch]
                oa_ch[ch] = term if oa_ch[ch] is None else oa_ch[ch] + term
        oa = jnp.concatenate(oa_ch, axis=1)

        def branch(o, idx, w_ref):
            gs = slice(idx * BRANCH_WIDTH, (idx + 1) * BRANCH_WIDTH)
            ms = slice(3 * BRANCH_WIDTH + idx * D_MODEL, 3 * BRANCH_WIDTH + (idx + 1) * D_MODEL)
            ms_b = slice(idx * D_MODEL, (idx + 1) * D_MODEL)
            gate = jnp.dot(h, wg_ref[:, gs], preferred_element_type=F32)
            gate = gate * jax.nn.sigmoid(gate)
            y = jnp.dot((o * gate).astype(BF16), w_ref[...], preferred_element_type=F32)
            mg = jnp.dot(h, wg_ref[:, ms], preferred_element_type=F32) + bm_ref[:, ms_b]
            return jax.nn.sigmoid(mg) * y

        merged = branch(oa, 0, wa_ref)
        merged = merged + branch(ob_ref[rows, :].astype(F32), 1, wb_ref)
        merged = merged + branch(om_ref[rows, :].astype(F32), 2, wm_ref)
        out = jnp.dot(merged.astype(BF16), wout_ref[...], preferred_element_type=F32)
        o_ref[rows, :] = xf + _rms_norm_rows(out, gpost_ref[...])


def _final(x2, oas, lses, ob, om, g_pre, w_g, b_merge, w_a, w_b, w_m, w_out, g_post, expand, dils, tm):
    m = x2.shape[0]
    per_tile = PERM_TILE // tm
    row = lambda n: pl.BlockSpec((tm, n), lambda i: (i, 0))
    full = lambda a: pl.BlockSpec(a.shape, lambda i: (0, 0), pipeline_mode=pl.Buffered(1))

    def group_operand(a, dil, width):
        if dil == 1:
            return a, row(width)
        cc = tm // dil
        view = a.reshape(m // PERM_TILE, dil, per_tile, cc, width)
        return view, pl.BlockSpec((None, dil, None, cc, width),
                                  lambda i: (i // per_tile, 0, i % per_tile, 0, 0))

    o_ops = [group_operand(a, dil, BRANCH_WIDTH) for a, dil in zip(oas, dils)]
    l_ops = [group_operand(a, dil, LANES) for a, dil in zip(lses, dils)]
    consts = (g_pre, w_g, b_merge, w_a, w_b, w_m, w_out, g_post, expand)
    kern = functools.partial(_final_kernel, tm=tm, dils=dils)
    n_ch = BRANCH_WIDTH // LANES
    return pl.pallas_call(
        kern,
        grid=(m // tm,),
        in_specs=([row(D_MODEL)] + [spec for _, spec in o_ops] + [spec for _, spec in l_ops]
                  + [row(BRANCH_WIDTH)] * 2 + [full(a) for a in consts]),
        out_specs=row(D_MODEL),
        out_shape=jax.ShapeDtypeStruct((m, D_MODEL), F32),
        scratch_shapes=[pltpu.VMEM((len(dils) * n_ch, tm, LANES), F32),
                        pltpu.VMEM((len(dils), tm, LANES), F32)],
        compiler_params=_cparams(("parallel",)),
        name="final",
    )(x2, *[a for a, _ in o_ops], *[a for a, _ in l_ops], ob, om, *consts)


def _rotary_tables(pos_per_lane, inv_per_lane, is_second_half):
    ang = pos_per_lane * inv_per_lane[None, :]
    cos, sin = jnp.cos(ang), jnp.sin(ang)
    zero = jnp.zeros_like(sin)
    return jnp.stack([cos,
                      jnp.where(is_second_half[None, :], sin, zero),
                      jnp.where(is_second_half[None, :], zero, -sin)]).astype(F32)


def _class_major_positions(s, dil):
    n = np.arange(s)
    within = n % PERM_TILE
    c = PERM_TILE // dil
    return (n // PERM_TILE) * PERM_TILE + (within % c) * dil + within // c


def _tables_a(s, dils):
    half = HEAD_DIM // 2
    lane = np.arange(LANES)
    inv = jnp.power(ROPE_THETA, -jnp.asarray(lane % half, F32) * 2.0 / HEAD_DIM)
    second = jnp.asarray((lane % HEAD_DIM) >= half)
    tables = []
    for dil in dils:
        pos = jnp.asarray(_class_major_positions(s, dil), F32)
        tables.append(_rotary_tables(jnp.broadcast_to(pos[:, None], (s, LANES)), inv, second))
    return jnp.stack(tables)


def _band_tables():
    q = np.arange(A_QBLK)[None, :, None]
    k = np.arange(A_KWIN)[None, None, :]
    delta = (np.arange(3) * A_HALF)[:, None, None]
    bias = np.where(np.abs(q - k + delta) <= A_HALF, 0.0, NEG_INF).astype(np.float32)
    eye2 = np.concatenate([np.eye(A_QBLK), np.eye(A_QBLK)], axis=0)
    return jnp.asarray(bias.transpose(0, 2, 1), BF16), jnp.asarray(eye2, BF16)


def _tables_b(s):
    dr = HEAD_DIM // 2
    half = dr // 2
    lane = np.arange(LANES)
    inv = jnp.power(ROPE_THETA, -jnp.asarray(lane % half, F32) * 2.0 / dr)
    row_pos = jnp.repeat(jnp.arange(s // GRID_W, dtype=F32), GRID_W)
    col_pos = (jnp.arange(s) % GRID_W).astype(F32)
    uses_col = jnp.asarray((lane % HEAD_DIM) >= dr)
    pos = jnp.where(uses_col[None, :], col_pos[:, None], row_pos[:, None])
    return _rotary_tables(pos, inv, jnp.asarray((lane % dr) >= half))


def kernel(x, mem, g_pre, w_in, b_merge, q_norm, k_norm, g_mem, w_mem_kv, w_br_a, w_br_b, w_br_m, w_out, g_post):
    b, s, d = x.shape
    n_mem = mem.shape[1]
    depth = g_pre.shape[0]
    n_groups = len(A_GROUPS)

    dils = tuple(dil for _, dil in A_GROUPS)
    assert s % PERM_TILE == 0
    t_a = _tables_a(s, dils)
    band_bias, eye2 = _band_tables()
    t_b = _tables_b(s)
    lane = np.arange(LANES)
    ones_bd = jnp.asarray((lane[:, None] // HEAD_DIM) == (lane[None, :] // HEAD_DIM), BF16)
    wide = np.arange(BRANCH_WIDTH)
    expand = jnp.asarray(lane[:, None] == (wide[None, :] // HEAD_DIM) * LSE_LANES_PER_HEAD, BF16)

    for layer in range(depth):
        w = w_in[layer]
        w_a = w[:, :A_QKV].reshape(d, 3, n_groups, BRANCH_WIDTH)
        w_a = w_a * jnp.asarray([HEAD_DIM ** -0.5, 1.0, 1.0], F32)[None, :, None, None]
        w_a = w_a.transpose(0, 2, 1, 3).reshape(d, A_QKV).astype(BF16)
        off = A_QKV
        w_bm = w[:, off:off + B_QKV + M_Q].astype(BF16)
        off += B_QKV + M_Q
        w_g = w[:, off:].astype(BF16)
        gq = jnp.tile(q_norm[layer] * HEAD_DIM ** -0.5, LANES // HEAD_DIM)[None, :]
        gk = jnp.tile(k_norm[layer], LANES // HEAD_DIM)[None, :]

        x2 = x.reshape(b * s, d)
        gp = g_pre[layer][None, :]
        qkv_a = _proj_a(x2, gp, w_a, t_a, s, dils)
        qkv_b, mq = _proj_bm(x2, gp, w_bm, t_b, gq, gk, ones_bd, s, tm=1024)
        kvm = _mem_kv(mem.reshape(b * n_mem, d), g_mem[layer][None, :],
                      w_mem_kv[layer].astype(BF16), tm=n_mem)

        oas, lses = [], []
        for gi, (_, dil) in enumerate(A_GROUPS):
            o_g, lse_g = _attn_a(qkv_a, band_bias, eye2, b, s, gi, dil)
            oas.append(o_g)
            lses.append(lse_g)
        ob = _attn_b(qkv_b, b, s, tq=512)
        om = _attn_m(mq, kvm, b, s, n_mem, tq=1024)

        x2 = _final(x2, oas, lses, ob, om, gp, w_g, b_merge[layer][None, :],
                    w_br_a[layer].astype(BF16), w_br_b[layer].astype(BF16),
                    w_br_m[layer].astype(BF16), w_out[layer].astype(BF16),
                    g_post[layer][None, :], expand, dils, tm=512)
        x = x2.reshape(b, s, d)
    return x
```

```python
import functools

import jax
import jax.numpy as jnp
import numpy as np
from jax import lax
from jax.experimental import pallas as pl
from jax.experimental.pallas import tpu as pltpu

D_MODEL = 1024
HEAD_DIM = 64
BRANCH_WIDTH = 512
A_GROUPS = ((128, 1), (512, 4), (2048, 16))
A_HEADS = 8
B_HEADS = 8
B_KV_HEADS = 2
M_HEADS = 4
M_HEAD_DIM = 128
GRID_W = 64
ROPE_THETA = 10000.0
NORM_EPS = 1e-6
NEG_INF = -1e30
N_BRANCHES = 3

A_QKV = 3 * len(A_GROUPS) * A_HEADS * HEAD_DIM
B_Q = B_HEADS * HEAD_DIM
B_KV = B_KV_HEADS * HEAD_DIM
B_QKV = B_Q + 2 * B_KV
M_Q = M_HEADS * M_HEAD_DIM
GATE_W = 3 * BRANCH_WIDTH + N_BRANCHES * D_MODEL

LANES = 128
A_QBLK = 128
A_KWIN = 256
A_HALF = 64
LSE_LANES_PER_HEAD = LANES // A_HEADS
PERM_TILE = 1024
ROW_CHUNK = 256
B_ROWS = 128
FINAL_ROWS = 256
VMEM_LIMIT = 56 * 1024 * 1024

F32 = jnp.float32
BF16 = jnp.bfloat16


def _cparams(sem):
    return pltpu.CompilerParams(dimension_semantics=sem, vmem_limit_bytes=VMEM_LIMIT)


def _rms_norm_rows(xf, g):
    return xf * lax.rsqrt(jnp.mean(xf * xf, axis=-1, keepdims=True) + NORM_EPS) * g


def _rotate(x, c, s_fwd, s_bwd, shift):
    return (x * c + pltpu.roll(x, shift, axis=1) * s_fwd
            + pltpu.roll(x, LANES - shift, axis=1) * s_bwd)


def _proj_a_kernel(x_ref, g_ref, w_ref, t_ref, o_ref, hf_ref, hp_ref, *, tm, dils):
    j = pl.program_id(1)
    n_lane = D_MODEL // LANES

    n_rot = 2 * BRANCH_WIDTH // LANES

    def prepare(gi):
        dil = dils[gi]
        c = tm // dil
        for r in range(dil):
            for ch in range(n_lane):
                hp_ref[gi, r * c:(r + 1) * c, ch * LANES:(ch + 1) * LANES] = (
                    hf_ref[ch, pl.ds(r, c, stride=dil), :].astype(BF16))

    def project(gi):
        for rc in range(tm // ROW_CHUNK):
            rows = slice(rc * ROW_CHUNK, (rc + 1) * ROW_CHUNK)
            acc = jnp.dot(hp_ref[gi, rows, :], w_ref[...], preferred_element_type=F32)
            c, sf, sb = t_ref[0, rows, :], t_ref[1, rows, :], t_ref[2, rows, :]
            for ch in range(n_rot):
                sl = slice(ch * LANES, (ch + 1) * LANES)
                o_ref[rows, sl] = _rotate(acc[:, sl], c, sf, sb, HEAD_DIM // 2).astype(BF16)
            o_ref[rows, n_rot * LANES:] = acc[:, n_rot * LANES:].astype(BF16)

    for gi in range(len(dils)):
        @pl.when(j == gi)
        def _(gi=gi):
            if gi == 0:
                h = _rms_norm_rows(x_ref[...], g_ref[...])
                for ch in range(n_lane):
                    hf_ref[ch] = h[:, ch * LANES:(ch + 1) * LANES]
                prepare(0)
            if gi + 1 < len(dils):
                prepare(gi + 1)
            project(gi)


def _proj_a(x2, g_pre, w_a, t_a, s, dils):
    m = x2.shape[0]
    tm = PERM_TILE
    n_sblk = s // tm
    group_w = 3 * BRANCH_WIDTH
    kern = functools.partial(_proj_a_kernel, tm=tm, dils=dils)
    return pl.pallas_call(
        kern,
        grid=(m // tm, len(dils)),
        in_specs=[
            pl.BlockSpec((tm, D_MODEL), lambda i, j: (i, 0)),
            pl.BlockSpec((1, D_MODEL), lambda i, j: (0, 0)),
            pl.BlockSpec((D_MODEL, group_w), lambda i, j: (0, j)),
            pl.BlockSpec((None, 3, tm, LANES), lambda i, j: (j, 0, i % n_sblk, 0)),
        ],
        out_specs=pl.BlockSpec((tm, group_w), lambda i, j: (i, j)),
        out_shape=jax.ShapeDtypeStruct((m, A_QKV), BF16),
        scratch_shapes=[pltpu.VMEM((D_MODEL // LANES, tm, LANES), F32),
                        pltpu.VMEM((len(dils), tm, D_MODEL), BF16)],
        compiler_params=_cparams(("parallel", "arbitrary")),
        name="proj_a",
    )(x2, g_pre, w_a, t_a)


def _proj_bm_kernel(x_ref, g_ref, w_ref, t_ref, gq_ref, gk_ref, ones_ref, ob_ref, om_ref):
    ones_bd = ones_ref[...]
    n_qk = (B_Q + B_KV) // LANES
    for rc in range(x_ref.shape[0] // ROW_CHUNK):
        rows = slice(rc * ROW_CHUNK, (rc + 1) * ROW_CHUNK)
        h = _rms_norm_rows(x_ref[rows, :], g_ref[...]).astype(BF16)
        acc = jnp.dot(h, w_ref[...], preferred_element_type=F32)
        c, sf, sb = t_ref[0, rows, :], t_ref[1, rows, :], t_ref[2, rows, :]
        for ch in range(n_qk):
            sl = slice(ch * LANES, (ch + 1) * LANES)
            y = acc[:, sl]
            sq = y * y
            hi = sq.astype(BF16)
            lo = (sq - hi.astype(F32)).astype(BF16)
            ss = (jnp.dot(hi, ones_bd, preferred_element_type=F32)
                  + jnp.dot(lo, ones_bd, preferred_element_type=F32))
            gain = gq_ref[...] if ch < B_Q // LANES else gk_ref[...]
            y = y * lax.rsqrt(ss * (1.0 / HEAD_DIM) + NORM_EPS) * gain
            ob_ref[rows, sl] = _rotate(y, c, sf, sb, HEAD_DIM // 4).astype(BF16)
        ob_ref[rows, B_Q + B_KV:] = acc[:, B_Q + B_KV:B_QKV].astype(BF16)
        om_ref[rows, :] = acc[:, B_QKV:].astype(BF16)


def _proj_bm(x2, g_pre, w_bm, t_b, gq, gk, ones_bd, s, tm):
    m = x2.shape[0]
    n_sblk = s // tm
    n = B_QKV + M_Q
    return pl.pallas_call(
        _proj_bm_kernel,
        grid=(m // tm,),
        in_specs=[
            pl.BlockSpec((tm, D_MODEL), lambda i: (i, 0)),
            pl.BlockSpec((1, D_MODEL), lambda i: (0, 0)),
            pl.BlockSpec((D_MODEL, n), lambda i: (0, 0)),
            pl.BlockSpec((3, tm, LANES), lambda i: (0, i % n_sblk, 0)),
            pl.BlockSpec((1, LANES), lambda i: (0, 0)),
            pl.BlockSpec((1, LANES), lambda i: (0, 0)),
            pl.BlockSpec((LANES, LANES), lambda i: (0, 0)),
        ],
        out_specs=[
            pl.BlockSpec((tm, B_QKV), lambda i: (i, 0)),
            pl.BlockSpec((tm, M_Q), lambda i: (i, 0)),
        ],
        out_shape=[jax.ShapeDtypeStruct((m, B_QKV), BF16),
                   jax.ShapeDtypeStruct((m, M_Q), BF16)],
        compiler_params=_cparams(("parallel",)),
        name="proj_bm",
    )(x2, g_pre, w_bm, t_b, gq, gk, ones_bd)


def _mem_kv_kernel(x_ref, g_ref, w_ref, o_ref):
    h = _rms_norm_rows(x_ref[...], g_ref[...]).astype(BF16)
    o_ref[...] = jnp.dot(h, w_ref[...], preferred_element_type=F32).astype(BF16)


def _mem_kv(mem2, g_mem, w_kv, tm):
    m = mem2.shape[0]
    n = w_kv.shape[1]
    return pl.pallas_call(
        _mem_kv_kernel,
        grid=(m // tm,),
        in_specs=[
            pl.BlockSpec((tm, D_MODEL), lambda i: (i, 0)),
            pl.BlockSpec((1, D_MODEL), lambda i: (0, 0)),
            pl.BlockSpec((D_MODEL, n), lambda i: (0, 0)),
        ],
        out_specs=pl.BlockSpec((tm, n), lambda i: (i, 0)),
        out_shape=jax.ShapeDtypeStruct((m, n), BF16),
        compiler_params=_cparams(("parallel",)),
        name="mem_kv",
    )(mem2, g_mem, w_kv)


def _band_consts():
    lane = lax.broadcasted_iota(jnp.int32, (A_QBLK, LANES), 1)
    return dict(first_half=lane < HEAD_DIM, head_of_lane=lane // LSE_LANES_PER_HEAD,
                ones=jnp.ones((A_KWIN, LANES), BF16))


def _band_block(cst, q_ref, k_ref, v_ref, bias_ref, eye_ref, o_ref, lse_ref, s_scr, p_scr,
                q_row, k_row, out_row, band_type):
    first_half, head_of_lane, ones = cst["first_half"], cst["head_of_lane"], cst["ones"]
    q_row = pl.multiple_of(q_row, A_QBLK)
    k_row = pl.multiple_of(k_row, A_HALF)
    out_row = pl.multiple_of(out_row, A_QBLK)
    bias_t = bias_ref[band_type]
    eye2 = eye_ref[...]
    n_pair = A_HEADS // 2
    nt = (((1,), (1,)), ((), ()))
    for j in range(n_pair):
        ps = slice(j * LANES, (j + 1) * LANES)
        qp = q_ref[pl.ds(q_row, A_QBLK), ps]
        kp = k_ref[pl.ds(k_row, A_KWIN), ps]
        zero = jnp.zeros_like(qp)
        q2 = jnp.concatenate([jnp.where(first_half, qp, zero), jnp.where(first_half, zero, qp)], axis=0)
        s_scr[j] = lax.dot_general(jnp.concatenate([q2, eye2], axis=1),
                                   jnp.concatenate([kp, bias_t], axis=1), nt,
                                   preferred_element_type=F32)
    mxs = []
    for j in range(n_pair):
        sc = s_scr[j]
        mx = jnp.max(sc, axis=-1, keepdims=True)
        p_scr[j] = jnp.exp(sc - mx).astype(BF16)
        mxs.append(mx)
    m_tile = jnp.zeros((A_QBLK, LANES), F32)
    l_tile = jnp.ones((A_QBLK, LANES), F32)
    for j in range(n_pair):
        ps = slice(j * LANES, (j + 1) * LANES)
        vp = v_ref[pl.ds(k_row, A_KWIN), ps]
        ov = jnp.dot(p_scr[j], jnp.concatenate([vp, ones], axis=1), preferred_element_type=F32)
        o_sel = jnp.where(first_half, ov[:A_QBLK, :LANES], ov[A_QBLK:, :LANES])
        l_sel = jnp.where(first_half, ov[:A_QBLK, LANES:], ov[A_QBLK:, LANES:])
        o_ref[pl.ds(out_row, A_QBLK), ps] = (o_sel / l_sel).astype(BF16)
        for e in range(2):
            rs = slice(e * A_QBLK, (e + 1) * A_QBLK)
            m_tile = jnp.where(head_of_lane == 2 * j + e, mxs[j][rs], m_tile)
            l_tile = jnp.where(head_of_lane == 2 * j + e, ov[rs, LANES:], l_tile)
    lse_ref[pl.ds(out_row, A_QBLK), :] = m_tile + jnp.log(l_tile)


def _window_start(q0, cls_len):
    return jnp.clip(q0 - A_HALF, 0, cls_len - A_KWIN)


def _attn_a_natural_kernel(q_ref, k_ref, v_ref, bias_ref, eye_ref, o_ref, lse_ref, s_scr, p_scr,
                           *, tq, cls_len):
    t = pl.program_id(1)
    cst = _band_consts()

    def body(qi, carry):
        q0 = t * tq + qi * A_QBLK
        k0 = _window_start(q0, cls_len)
        _band_block(cst, q_ref, k_ref, v_ref, bias_ref, eye_ref, o_ref, lse_ref, s_scr, p_scr,
                    qi * A_QBLK, k0, qi * A_QBLK, (q0 - k0) // A_HALF)
        return carry

    lax.fori_loop(0, tq // A_QBLK, body, 0, unroll=2)


def _attn_a_classes_kernel(q_ref, k_ref, v_ref, bias_ref, eye_ref, o_ref, lse_ref,
                           qs, ks, vs, os_, ls_, s_scr, p_scr, *, n_tiles, n_cls, c, cls_len):
    for t in range(n_tiles):
        for cl in range(n_cls):
            rows = slice(cl * cls_len + t * c, cl * cls_len + (t + 1) * c)
            qs[rows, :] = q_ref[t, cl]
            ks[rows, :] = k_ref[t, cl]
            vs[rows, :] = v_ref[t, cl]
    cst = _band_consts()
    nq = cls_len // A_QBLK

    def body(blk, carry):
        base = (blk // nq) * cls_len
        q0 = (blk % nq) * A_QBLK
        k0 = _window_start(q0, cls_len)
        _band_block(cst, qs, ks, vs, bias_ref, eye_ref, os_, ls_, s_scr, p_scr,
                    base + q0, base + k0, base + q0, (q0 - k0) // A_HALF)
        return carry

    lax.fori_loop(0, n_cls * nq, body, 0, unroll=2)
    for t in range(n_tiles):
        for cl in range(n_cls):
            rows = slice(cl * cls_len + t * c, cl * cls_len + (t + 1) * c)
            o_ref[t, cl] = os_[rows, :]
            lse_ref[t, cl] = ls_[rows, :]


def _attn_a(qkv_a, band_bias, eye2, b, s, gi, dil):
    cls_len = s // dil
    assert cls_len >= A_KWIN and cls_len % A_QBLK == 0 and (cls_len // A_QBLK) % 2 == 0
    stage = [pltpu.VMEM((A_HEADS // 2, 2 * A_QBLK, A_KWIN), F32),
             pltpu.VMEM((A_HEADS // 2, 2 * A_QBLK, A_KWIN), BF16)]
    const_specs = [pl.BlockSpec(band_bias.shape, lambda *_: (0, 0, 0)),
                   pl.BlockSpec(eye2.shape, lambda *_: (0, 0))]
    if dil == 1:
        tq = min(cls_len, 1024)
        view = qkv_a.reshape(b, s, A_QKV)
        kern = functools.partial(_attn_a_natural_kernel, tq=tq, cls_len=cls_len)
        o, lse = pl.pallas_call(
            kern,
            grid=(b, cls_len // tq),
            in_specs=[
                pl.BlockSpec((None, tq, BRANCH_WIDTH), lambda bi, t: (bi, t, 3 * gi)),
                pl.BlockSpec((None, cls_len, BRANCH_WIDTH), lambda bi, t: (bi, 0, 3 * gi + 1)),
                pl.BlockSpec((None, cls_len, BRANCH_WIDTH), lambda bi, t: (bi, 0, 3 * gi + 2)),
            ] + const_specs,
            out_specs=[
                pl.BlockSpec((None, tq, BRANCH_WIDTH), lambda bi, t: (bi, t, 0)),
                pl.BlockSpec((None, tq, LANES), lambda bi, t: (bi, t, 0)),
            ],
            out_shape=[jax.ShapeDtypeStruct((b, s, BRANCH_WIDTH), BF16),
                       jax.ShapeDtypeStruct((b, s, LANES), F32)],
            scratch_shapes=stage,
            compiler_params=_cparams(("parallel", "arbitrary")),
            name=f"attn_a{gi}",
        )(view, view, view, band_bias, eye2)
    else:
        n_tiles = s // PERM_TILE
        c = PERM_TILE // dil
        n_cls = max(1, 1024 // cls_len)
        assert dil % n_cls == 0 and c % 16 == 0
        view = qkv_a.reshape(b, n_tiles, dil, c, A_QKV)
        kern = functools.partial(_attn_a_classes_kernel, n_tiles=n_tiles, n_cls=n_cls, c=c, cls_len=cls_len)
        rows = n_cls * cls_len
        blk = lambda width: (None, n_tiles, n_cls, c, width)
        o, lse = pl.pallas_call(
            kern,
            grid=(b, dil // n_cls),
            in_specs=[
                pl.BlockSpec(blk(BRANCH_WIDTH), lambda bi, r: (bi, 0, r, 0, 3 * gi)),
                pl.BlockSpec(blk(BRANCH_WIDTH), lambda bi, r: (bi, 0, r, 0, 3 * gi + 1)),
                pl.BlockSpec(blk(BRANCH_WIDTH), lambda bi, r: (bi, 0, r, 0, 3 * gi + 2)),
            ] + const_specs,
            out_specs=[
                pl.BlockSpec(blk(BRANCH_WIDTH), lambda bi, r: (bi, 0, r, 0, 0)),
                pl.BlockSpec(blk(LANES), lambda bi, r: (bi, 0, r, 0, 0)),
            ],
            out_shape=[jax.ShapeDtypeStruct((b, n_tiles, dil, c, BRANCH_WIDTH), BF16),
                       jax.ShapeDtypeStruct((b, n_tiles, dil, c, LANES), F32)],
            scratch_shapes=[pltpu.VMEM((rows, BRANCH_WIDTH), BF16)] * 4
                           + [pltpu.VMEM((rows, LANES), F32)] + stage,
            compiler_params=_cparams(("parallel", "arbitrary")),
            name=f"attn_a{gi}",
        )(view, view, view, band_bias, eye2)
    return o.reshape(b * s, BRANCH_WIDTH), lse.reshape(b * s, LANES)


def _attn_b_kernel(q_ref, k_ref, v_ref, o_ref, kdup, vext, *, tq):
    s = k_ref.shape[0]

    @pl.when(pl.program_id(1) == 0)
    def _():
        k = k_ref[...]
        v = v_ref[...]
        ones = jnp.ones((s, LANES), BF16)
        for kh in range(B_KV_HEADS):
            ks = slice(kh * HEAD_DIM, (kh + 1) * HEAD_DIM)
            kdup[kh] = jnp.concatenate([k[:, ks], k[:, ks]], axis=1)
            vext[kh] = jnp.concatenate([v[:, ks], v[:, ks], ones], axis=1)

    sub = B_ROWS
    first_half = lax.broadcasted_iota(jnp.int32, (sub, LANES), 1) < HEAD_DIM
    pairs_per_kv = B_HEADS // B_KV_HEADS // 2
    nt = (((1,), (1,)), ((), ()))
    for hh in range(tq // sub):
        rows = slice(hh * sub, (hh + 1) * sub)
        for j in range(B_HEADS // 2):
            kh = j // pairs_per_kv
            ps = slice(j * LANES, (j + 1) * LANES)
            qp = q_ref[rows, ps]
            zero = jnp.zeros_like(qp)
            q2 = jnp.concatenate([jnp.where(first_half, qp, zero), jnp.where(first_half, zero, qp)], axis=0)
            sc = lax.dot_general(q2, kdup[kh], nt, preferred_element_type=F32)
            mx = jnp.max(sc, axis=-1, keepdims=True)
            p = jnp.exp(sc - mx).astype(BF16)
            ov = jnp.dot(p, vext[kh], preferred_element_type=F32)
            o_sel = jnp.where(first_half, ov[:sub, :LANES], ov[sub:, :LANES])
            l_sel = jnp.where(first_half, ov[:sub, LANES:], ov[sub:, LANES:])
            o_ref[rows, ps] = (o_sel / l_sel).astype(BF16)


def _attn_b(qkv_b, b, s, tq):
    view = qkv_b.reshape(b, s, B_QKV)
    o = pl.pallas_call(
        functools.partial(_attn_b_kernel, tq=tq),
        grid=(b, s // tq),
        in_specs=[
            pl.BlockSpec((None, tq, B_Q), lambda bi, t: (bi, t, 0)),
            pl.BlockSpec((None, s, B_KV), lambda bi, t: (bi, 0, B_Q // B_KV)),
            pl.BlockSpec((None, s, B_KV), lambda bi, t: (bi, 0, B_Q // B_KV + 1)),
        ],
        out_specs=pl.BlockSpec((None, tq, B_Q), lambda bi, t: (bi, t, 0)),
        out_shape=jax.ShapeDtypeStruct((b, s, B_Q), BF16),
        scratch_shapes=[pltpu.VMEM((B_KV_HEADS, s, LANES), BF16),
                        pltpu.VMEM((B_KV_HEADS, s, 2 * LANES), BF16)],
        compiler_params=_cparams(("parallel", "arbitrary")),
        name="attn_b",
    )(view, view, view)
    return o.reshape(b * s, B_Q)


def _attn_m_kernel(q_ref, k_ref, v_ref, o_ref):
    scale = M_HEAD_DIM ** -0.5
    outs = []
    for h in range(M_HEADS):
        hs = slice(h * M_HEAD_DIM, (h + 1) * M_HEAD_DIM)
        sc = lax.dot_general(q_ref[:, hs], k_ref[:, hs], (((1,), (1,)), ((), ())),
                             preferred_element_type=F32) * scale
        mx = jnp.max(sc, axis=-1, keepdims=True)
        p = jnp.exp(sc - mx)
        l = jnp.sum(p, axis=-1, keepdims=True)
        o = jnp.dot(p.astype(BF16), v_ref[:, hs], preferred_element_type=F32)
        outs.append(o / l)
    o_ref[...] = jnp.concatenate(outs, axis=1).astype(BF16)


def _attn_m(mq, kvm, b, s, n_mem, tq):
    qv = mq.reshape(b, s, M_Q)
    kv = kvm.reshape(b, n_mem, 2 * M_Q)
    o = pl.pallas_call(
        _attn_m_kernel,
        grid=(b, s // tq),
        in_specs=[
            pl.BlockSpec((None, tq, M_Q), lambda bi, t: (bi, t, 0)),
            pl.BlockSpec((None, n_mem, M_Q), lambda bi, t: (bi, 0, 0)),
            pl.BlockSpec((None, n_mem, M_Q), lambda bi, t: (bi, 0, 1)),
        ],
        out_specs=pl.BlockSpec((None, tq, M_Q), lambda bi, t: (bi, t, 0)),
        out_shape=jax.ShapeDtypeStruct((b, s, M_Q), BF16),
        compiler_params=_cparams(("parallel", "arbitrary")),
        name="attn_m",
    )(qv, kv, kv)
    return o.reshape(b * s, M_Q)


def _final_kernel(x_ref, oa0_ref, oa1_ref, oa2_ref, l0_ref, l1_ref, l2_ref, ob_ref, om_ref,
                  gpre_ref, wg_ref, bm_ref, wa_ref, wb_ref, wm_ref, wout_ref, gpost_ref,
                  exp_ref, o_ref, nat_o, nat_l, *, tm, dils):
    n_ch = BRANCH_WIDTH // LANES
    group_refs = list(zip(dils, (oa0_ref, oa1_ref, oa2_ref), (l0_ref, l1_ref, l2_ref)))

    for gi, (dil, og_ref, l_ref) in enumerate(group_refs):
        if dil == 1:
            continue
        cc = tm // dil
        for r in range(dil):
            o = og_ref[r].astype(F32)
            for ch in range(n_ch):
                nat_o[gi * n_ch + ch, pl.ds(r, cc, stride=dil), :] = o[:, ch * LANES:(ch + 1) * LANES]
            nat_l[gi, pl.ds(r, cc, stride=dil), :] = l_ref[r]

    expand = exp_ref[...]
    for sb in range(tm // FINAL_ROWS):
        rows = slice(sb * FINAL_ROWS, (sb + 1) * FINAL_ROWS)
        xf = x_ref[rows, :]
        h = _rms_norm_rows(xf, gpre_ref[...]).astype(BF16)
        og, lses = [], []
        for gi, (dil, og_ref, l_ref) in enumerate(group_refs):
            if dil == 1:
                o = og_ref[rows, :].astype(F32)
                og.append([o[:, ch * LANES:(ch + 1) * LANES] for ch in range(n_ch)])
                lses.append(l_ref[rows, :])
            else:
                og.append([nat_o[gi * n_ch + ch, rows, :] for ch in range(n_ch)])
                lses.append(nat_l[gi, rows, :])

        mx = functools.reduce(jnp.maximum, lses)
        es = [jnp.exp(l - mx) for l in lses]
        inv_z = 1.0 / functools.reduce(lambda a, c: a + c, es)
        oa_ch = [None] * n_ch
        rest = None
        for gi, (e, o_chunks) in enumerate(zip(es, og)):
            if gi + 1 < len(es):
                w = e * inv_z
                hi = w.astype(BF16)
                lo = (w - hi.astype(F32)).astype(BF16)
                w_wide = (jnp.dot(hi, expand, preferred_element_type=F32)
                          + jnp.dot(lo, expand, preferred_element_type=F32))
                rest = 1.0 - w_wide if rest is None else rest - w_wide
            else:
                w_wide = rest
            for ch in range(n_ch):
                term = w_wide[:, ch * LANES:(ch + 1) * LANES] * o_chunks[ch]
                oa_ch[ch] = term if oa_ch[ch] is None else oa_ch[ch] + term
        oa = jnp.concatenate(oa_ch, axis=1)

        def branch(o, idx, w_ref):
            gs = slice(idx * BRANCH_WIDTH, (idx + 1) * BRANCH_WIDTH)
            ms = slice(3 * BRANCH_WIDTH + idx * D_MODEL, 3 * BRANCH_WIDTH + (idx + 1) * D_MODEL)
            ms_b = slice(idx * D_MODEL, (idx + 1) * D_MODEL)
            gate = jnp.dot(h, wg_ref[:, gs], preferred_element_type=F32)
            gate = gate * jax.nn.sigmoid(gate)
            y = jnp.dot((o * gate).astype(BF16), w_ref[...], preferred_element_type=F32)
            mg = jnp.dot(h, wg_ref[:, ms], preferred_element_type=F32) + bm_ref[:, ms_b]
            return jax.nn.sigmoid(mg) * y

        merged = branch(oa, 0, wa_ref)
        merged = merged + branch(ob_ref[rows, :].astype(F32), 1, wb_ref)
        merged = merged + branch(om_ref[rows, :].astype(F32), 2, wm_ref)
        out = jnp.dot(merged.astype(BF16), wout_ref[...], preferred_element_type=F32)
        o_ref[rows, :] = xf + _rms_norm_rows(out, gpost_ref[...])


def _final(x2, oas, lses, ob, om, g_pre, w_g, b_merge, w_a, w_b, w_m, w_out, g_post, expand, dils, tm):
    m = x2.shape[0]
    per_tile = PERM_TILE // tm
    row = lambda n: pl.BlockSpec((tm, n), lambda i: (i, 0))
    full = lambda a: pl.BlockSpec(a.shape, lambda i: (0, 0), pipeline_mode=pl.Buffered(1))

    def group_operand(a, dil, width):
        if dil == 1:
            return a, row(width)
        cc = tm // dil
        view = a.reshape(m // PERM_TILE, dil, per_tile, cc, width)
        return view, pl.BlockSpec((None, dil, None, cc, width),
                                  lambda i: (i // per_tile, 0, i % per_tile, 0, 0))

    o_ops = [group_operand(a, dil, BRANCH_WIDTH) for a, dil in zip(oas, dils)]
    l_ops = [group_operand(a, dil, LANES) for a, dil in zip(lses, dils)]
    consts = (g_pre, w_g, b_merge, w_a, w_b, w_m, w_out, g_post, expand)
    kern = functools.partial(_final_kernel, tm=tm, dils=dils)
    n_ch = BRANCH_WIDTH // LANES
    return pl.pallas_call(
        kern,
        grid=(m // tm,),
        in_specs=([row(D_MODEL)] + [spec for _, spec in o_ops] + [spec for _, spec in l_ops]
                  + [row(BRANCH_WIDTH)] * 2 + [full(a) for a in consts]),
        out_specs=row(D_MODEL),
        out_shape=jax.ShapeDtypeStruct((m, D_MODEL), F32),
        scratch_shapes=[pltpu.VMEM((len(dils) * n_ch, tm, LANES), F32),
                        pltpu.VMEM((len(dils), tm, LANES), F32)],
        compiler_params=_cparams(("parallel",)),
        name="final",
    )(x2, *[a for a, _ in o_ops], *[a for a, _ in l_ops], ob, om, *consts)


def _rotary_tables(pos_per_lane, inv_per_lane, is_second_half):
    ang = pos_per_lane * inv_per_lane[None, :]
    cos, sin = jnp.cos(ang), jnp.sin(ang)
    zero = jnp.zeros_like(sin)
    return jnp.stack([cos,
                      jnp.where(is_second_half[None, :], sin, zero),
                      jnp.where(is_second_half[None, :], zero, -sin)]).astype(F32)


def _class_major_positions(s, dil):
    n = np.arange(s)
    within = n % PERM_TILE
    c = PERM_TILE // dil
    return (n // PERM_TILE) * PERM_TILE + (within % c) * dil + within // c


def _tables_a(s, dils):
    half = HEAD_DIM // 2
    lane = np.arange(LANES)
    inv = jnp.power(ROPE_THETA, -jnp.asarray(lane % half, F32) * 2.0 / HEAD_DIM)
    second = jnp.asarray((lane % HEAD_DIM) >= half)
    tables = []
    for dil in dils:
        pos = jnp.asarray(_class_major_positions(s, dil), F32)
        tables.append(_rotary_tables(jnp.broadcast_to(pos[:, None], (s, LANES)), inv, second))
    return jnp.stack(tables)


def _band_tables():
    q = np.arange(A_QBLK)[None, :, None]
    k = np.arange(A_KWIN)[None, None, :]
    delta = (np.arange(3) * A_HALF)[:, None, None]
    bias = np.where(np.abs(q - k + delta) <= A_HALF, 0.0, NEG_INF).astype(np.float32)
    eye2 = np.concatenate([np.eye(A_QBLK), np.eye(A_QBLK)], axis=0)
    return jnp.asarray(bias.transpose(0, 2, 1), BF16), jnp.asarray(eye2, BF16)


def _tables_b(s):
    dr = HEAD_DIM // 2
    half = dr // 2
    lane = np.arange(LANES)
    inv = jnp.power(ROPE_THETA, -jnp.asarray(lane % half, F32) * 2.0 / dr)
    row_pos = jnp.repeat(jnp.arange(s // GRID_W, dtype=F32), GRID_W)
    col_pos = (jnp.arange(s) % GRID_W).astype(F32)
    uses_col = jnp.asarray((lane % HEAD_DIM) >= dr)
    pos = jnp.where(uses_col[None, :], col_pos[:, None], row_pos[:, None])
    return _rotary_tables(pos, inv, jnp.asarray((lane % dr) >= half))


def kernel(x, mem, g_pre, w_in, b_merge, q_norm, k_norm, g_mem, w_mem_kv, w_br_a, w_br_b, w_br_m, w_out, g_post):
    b, s, d = x.shape
    n_mem = mem.shape[1]
    depth = g_pre.shape[0]
    n_groups = len(A_GROUPS)

    dils = tuple(dil for _, dil in A_GROUPS)
    assert s % PERM_TILE == 0
    t_a = _tables_a(s, dils)
    band_bias, eye2 = _band_tables()
    t_b = _tables_b(s)
    lane = np.arange(LANES)
    ones_bd = jnp.asarray((lane[:, None] // HEAD_DIM) == (lane[None, :] // HEAD_DIM), BF16)
    wide = np.arange(BRANCH_WIDTH)
    expand = jnp.asarray(lane[:, None] == (wide[None, :] // HEAD_DIM) * LSE_LANES_PER_HEAD, BF16)

    for layer in range(depth):
        w = w_in[layer]
        w_a = w[:, :A_QKV].reshape(d, 3, n_groups, BRANCH_WIDTH)
        w_a = w_a * jnp.asarray([HEAD_DIM ** -0.5, 1.0, 1.0], F32)[None, :, None, None]
        w_a = w_a.transpose(0, 2, 1, 3).reshape(d, A_QKV).astype(BF16)
        off = A_QKV
        w_bm = w[:, off:off + B_QKV + M_Q].astype(BF16)
        off += B_QKV + M_Q
        w_g = w[:, off:].astype(BF16)
        gq = jnp.tile(q_norm[layer] * HEAD_DIM ** -0.5, LANES // HEAD_DIM)[None, :]
        gk = jnp.tile(k_norm[layer], LANES // HEAD_DIM)[None, :]

        x2 = x.reshape(b * s, d)
        gp = g_pre[layer][None, :]
        qkv_a = _proj_a(x2, gp, w_a, t_a, s, dils)
        qkv_b, mq = _proj_bm(x2, gp, w_bm, t_b, gq, gk, ones_bd, s, tm=1024)
        kvm = _mem_kv(mem.reshape(b * n_mem, d), g_mem[layer][None, :],
                      w_mem_kv[layer].astype(BF16), tm=n_mem)

        oas, lses = [], []
        for gi, (_, dil) in enumerate(A_GROUPS):
            o_g, lse_g = _attn_a(qkv_a, band_bias, eye2, b, s, gi, dil)
            oas.append(o_g)
            lses.append(lse_g)
        ob = _attn_b(qkv_b, b, s, tq=512)
        om = _attn_m(mq, kvm, b, s, n_mem, tq=1024)

        x2 = _final(x2, oas, lses, ob, om, gp, w_g, b_merge[layer][None, :],
                    w_br_a[layer].astype(BF16), w_br_b[layer].astype(BF16),
                    w_br_m[layer].astype(BF16), w_out[layer].astype(BF16),
                    g_post[layer][None, :], expand, dils, tm=512)
        x = x2.reshape(b, s, d)
    return x
```

```python
import functools

import jax
import jax.numpy as jnp
import numpy as np
from jax import lax
from jax.experimental import pallas as pl
from jax.experimental.pallas import tpu as pltpu

D_MODEL = 1024
HEAD_DIM = 64
BRANCH_WIDTH = 512
A_GROUPS = ((128, 1), (512, 4), (2048, 16))
A_HEADS = 8
B_HEADS = 8
B_KV_HEADS = 2
M_HEADS = 4
M_HEAD_DIM = 128
GRID_W = 64
ROPE_THETA = 10000.0
NORM_EPS = 1e-6
NEG_INF = -1e30
N_BRANCHES = 3

A_QKV = 3 * len(A_GROUPS) * A_HEADS * HEAD_DIM
B_Q = B_HEADS * HEAD_DIM
B_KV = B_KV_HEADS * HEAD_DIM
B_QKV = B_Q + 2 * B_KV
M_Q = M_HEADS * M_HEAD_DIM
GATE_W = 3 * BRANCH_WIDTH + N_BRANCHES * D_MODEL

LANES = 128
A_QBLK = 128
A_KWIN = 256
A_HALF = 64
LSE_LANES_PER_HEAD = LANES // A_HEADS
PERM_TILE = 1024
ROW_CHUNK = 256
B_ROWS = 128
FINAL_ROWS = 256
VMEM_LIMIT = 56 * 1024 * 1024

F32 = jnp.float32
BF16 = jnp.bfloat16


def _cparams(sem):
    return pltpu.CompilerParams(dimension_semantics=sem, vmem_limit_bytes=VMEM_LIMIT)


def _rms_norm_rows(xf, g):
    return xf * lax.rsqrt(jnp.mean(xf * xf, axis=-1, keepdims=True) + NORM_EPS) * g


def _rotate(x, c, s_fwd, s_bwd, shift):
    return (x * c + pltpu.roll(x, shift, axis=1) * s_fwd
            + pltpu.roll(x, LANES - shift, axis=1) * s_bwd)


def _proj_a_kernel(x_ref, g_ref, w_ref, t_ref, o_ref, hf_ref, hp_ref, *, tm, dils):
    j = pl.program_id(1)
    n_lane = D_MODEL // LANES

    n_rot = 2 * BRANCH_WIDTH // LANES

    def prepare(gi):
        dil = dils[gi]
        c = tm // dil
        for r in range(dil):
            for ch in range(n_lane):
                hp_ref[gi, r * c:(r + 1) * c, ch * LANES:(ch + 1) * LANES] = (
                    hf_ref[ch, pl.ds(r, c, stride=dil), :].astype(BF16))

    def project(gi):
        for rc in range(tm // ROW_CHUNK):
            rows = slice(rc * ROW_CHUNK, (rc + 1) * ROW_CHUNK)
            acc = jnp.dot(hp_ref[gi, rows, :], w_ref[...], preferred_element_type=F32)
            c, sf, sb = t_ref[0, rows, :], t_ref[1, rows, :], t_ref[2, rows, :]
            for ch in range(n_rot):
                sl = slice(ch * LANES, (ch + 1) * LANES)
                o_ref[rows, sl] = _rotate(acc[:, sl], c, sf, sb, HEAD_DIM // 2).astype(BF16)
            o_ref[rows, n_rot * LANES:] = acc[:, n_rot * LANES:].astype(BF16)

    for gi in range(len(dils)):
        @pl.when(j == gi)
        def _(gi=gi):
            if gi == 0:
                h = _rms_norm_rows(x_ref[...], g_ref[...])
                for ch in range(n_lane):
                    hf_ref[ch] = h[:, ch * LANES:(ch + 1) * LANES]
                prepare(0)
            if gi + 1 < len(dils):
                prepare(gi + 1)
            project(gi)


def _proj_a(x2, g_pre, w_a, t_a, s, dils):
    m = x2.shape[0]
    tm = PERM_TILE
    n_sblk = s // tm
    group_w = 3 * BRANCH_WIDTH
    kern = functools.partial(_proj_a_kernel, tm=tm, dils=dils)
    return pl.pallas_call(
        kern,
        grid=(m // tm, len(dils)),
        in_specs=[
            pl.BlockSpec((tm, D_MODEL), lambda i, j: (i, 0)),
            pl.BlockSpec((1, D_MODEL), lambda i, j: (0, 0)),
            pl.BlockSpec((D_MODEL, group_w), lambda i, j: (0, j)),
            pl.BlockSpec((None, 3, tm, LANES), lambda i, j: (j, 0, i % n_sblk, 0)),
        ],
        out_specs=pl.BlockSpec((tm, group_w), lambda i, j: (i, j)),
        out_shape=jax.ShapeDtypeStruct((m, A_QKV), BF16),
        scratch_shapes=[pltpu.VMEM((D_MODEL // LANES, tm, LANES), F32),
                        pltpu.VMEM((len(dils), tm, D_MODEL), BF16)],
        compiler_params=_cparams(("parallel", "arbitrary")),
        name="proj_a",
    )(x2, g_pre, w_a, t_a)


def _proj_bm_kernel(x_ref, g_ref, w_ref, t_ref, gq_ref, gk_ref, ones_ref, ob_ref, om_ref):
    ones_bd = ones_ref[...]
    n_qk = (B_Q + B_KV) // LANES
    for rc in range(x_ref.shape[0] // ROW_CHUNK):
        rows = slice(rc * ROW_CHUNK, (rc + 1) * ROW_CHUNK)
        h = _rms_norm_rows(x_ref[rows, :], g_ref[...]).astype(BF16)
        acc = jnp.dot(h, w_ref[...], preferred_element_type=F32)
        c, sf, sb = t_ref[0, rows, :], t_ref[1, rows, :], t_ref[2, rows, :]
        for ch in range(n_qk):
            sl = slice(ch * LANES, (ch + 1) * LANES)
            y = acc[:, sl]
            sq = y * y
            hi = sq.astype(BF16)
            lo = (sq - hi.astype(F32)).astype(BF16)
            ss = (jnp.dot(hi, ones_bd, preferred_element_type=F32)
                  + jnp.dot(lo, ones_bd, preferred_element_type=F32))
            gain = gq_ref[...] if ch < B_Q // LANES else gk_ref[...]
            y = y * lax.rsqrt(ss * (1.0 / HEAD_DIM) + NORM_EPS) * gain
            ob_ref[rows, sl] = _rotate(y, c, sf, sb, HEAD_DIM // 4).astype(BF16)
        ob_ref[rows, B_Q + B_KV:] = acc[:, B_Q + B_KV:B_QKV].astype(BF16)
        om_ref[rows, :] = acc[:, B_QKV:].astype(BF16)


def _proj_bm(x2, g_pre, w_bm, t_b, gq, gk, ones_bd, s, tm):
    m = x2.shape[0]
    n_sblk = s // tm
    n = B_QKV + M_Q
    return pl.pallas_call(
        _proj_bm_kernel,
        grid=(m // tm,),
        in_specs=[
            pl.BlockSpec((tm, D_MODEL), lambda i: (i, 0)),
            pl.BlockSpec((1, D_MODEL), lambda i: (0, 0)),
            pl.BlockSpec((D_MODEL, n), lambda i: (0, 0)),
            pl.BlockSpec((3, tm, LANES), lambda i: (0, i % n_sblk, 0)),
            pl.BlockSpec((1, LANES), lambda i: (0, 0)),
            pl.BlockSpec((1, LANES), lambda i: (0, 0)),
            pl.BlockSpec((LANES, LANES), lambda i: (0, 0)),
        ],
        out_specs=[
            pl.BlockSpec((tm, B_QKV), lambda i: (i, 0)),
            pl.BlockSpec((tm, M_Q), lambda i: (i, 0)),
        ],
        out_shape=[jax.ShapeDtypeStruct((m, B_QKV), BF16),
                   jax.ShapeDtypeStruct((m, M_Q), BF16)],
        compiler_params=_cparams(("parallel",)),
        name="proj_bm",
    )(x2, g_pre, w_bm, t_b, gq, gk, ones_bd)


def _mem_kv_kernel(x_ref, g_ref, w_ref, o_ref):
    h = _rms_norm_rows(x_ref[...], g_ref[...]).astype(BF16)
    o_ref[...] = jnp.dot(h, w_ref[...], preferred_element_type=F32).astype(BF16)


def _mem_kv(mem2, g_mem, w_kv, tm):
    m = mem2.shape[0]
    n = w_kv.shape[1]
    return pl.pallas_call(
        _mem_kv_kernel,
        grid=(m // tm,),
        in_specs=[
            pl.BlockSpec((tm, D_MODEL), lambda i: (i, 0)),
            pl.BlockSpec((1, D_MODEL), lambda i: (0, 0)),
            pl.BlockSpec((D_MODEL, n), lambda i: (0, 0)),
        ],
        out_specs=pl.BlockSpec((tm, n), lambda i: (i, 0)),
        out_shape=jax.ShapeDtypeStruct((m, n), BF16),
        compiler_params=_cparams(("parallel",)),
        name="mem_kv",
    )(mem2, g_mem, w_kv)


def _band_consts():
    lane = lax.broadcasted_iota(jnp.int32, (A_QBLK, LANES), 1)
    return dict(first_half=lane < HEAD_DIM, head_of_lane=lane // LSE_LANES_PER_HEAD,
                ones=jnp.ones((A_KWIN, LANES), BF16))


def _band_block(cst, q_ref, k_ref, v_ref, bias_ref, eye_ref, o_ref, lse_ref, s_scr, p_scr,
                q_row, k_row, out_row, band_type):
    first_half, head_of_lane, ones = cst["first_half"], cst["head_of_lane"], cst["ones"]
    q_row = pl.multiple_of(q_row, A_QBLK)
    k_row = pl.multiple_of(k_row, A_HALF)
    out_row = pl.multiple_of(out_row, A_QBLK)
    bias_t = bias_ref[band_type]
    eye2 = eye_ref[...]
    n_pair = A_HEADS // 2
    nt = (((1,), (1,)), ((), ()))
    for j in range(n_pair):
        ps = slice(j * LANES, (j + 1) * LANES)
        qp = q_ref[pl.ds(q_row, A_QBLK), ps]
        kp = k_ref[pl.ds(k_row, A_KWIN), ps]
        zero = jnp.zeros_like(qp)
        q2 = jnp.concatenate([jnp.where(first_half, qp, zero), jnp.where(first_half, zero, qp)], axis=0)
        s_scr[j] = lax.dot_general(jnp.concatenate([q2, eye2], axis=1),
                                   jnp.concatenate([kp, bias_t], axis=1), nt,
                                   preferred_element_type=F32)
    mxs = []
    for j in range(n_pair):
        sc = s_scr[j]
        mx = jnp.max(sc, axis=-1, keepdims=True)
        p_scr[j] = jnp.exp(sc - mx).astype(BF16)
        mxs.append(mx)
    m_tile = jnp.zeros((A_QBLK, LANES), F32)
    l_tile = jnp.ones((A_QBLK, LANES), F32)
    for j in range(n_pair):
        ps = slice(j * LANES, (j + 1) * LANES)
        vp = v_ref[pl.ds(k_row, A_KWIN), ps]
        ov = jnp.dot(p_scr[j], jnp.concatenate([vp, ones], axis=1), preferred_element_type=F32)
        o_sel = jnp.where(first_half, ov[:A_QBLK, :LANES], ov[A_QBLK:, :LANES])
        l_sel = jnp.where(first_half, ov[:A_QBLK, LANES:], ov[A_QBLK:, LANES:])
        o_ref[pl.ds(out_row, A_QBLK), ps] = (o_sel / l_sel).astype(BF16)
        for e in range(2):
            rs = slice(e * A_QBLK, (e + 1) * A_QBLK)
            m_tile = jnp.where(head_of_lane == 2 * j + e, mxs[j][rs], m_tile)
            l_tile = jnp.where(head_of_lane == 2 * j + e, ov[rs, LANES:], l_tile)
    lse_ref[pl.ds(out_row, A_QBLK), :] = m_tile + jnp.log(l_tile)


def _window_start(q0, cls_len):
    return jnp.clip(q0 - A_HALF, 0, cls_len - A_KWIN)


def _attn_a_natural_kernel(q_ref, k_ref, v_ref, bias_ref, eye_ref, o_ref, lse_ref, s_scr, p_scr,
                           *, tq, cls_len):
    t = pl.program_id(1)
    cst = _band_consts()

    def body(qi, carry):
        q0 = t * tq + qi * A_QBLK
        k0 = _window_start(q0, cls_len)
        _band_block(cst, q_ref, k_ref, v_ref, bias_ref, eye_ref, o_ref, lse_ref, s_scr, p_scr,
                    qi * A_QBLK, k0, qi * A_QBLK, (q0 - k0) // A_HALF)
        return carry

    lax.fori_loop(0, tq // A_QBLK, body, 0, unroll=2)


def _attn_a_classes_kernel(q_ref, k_ref, v_ref, bias_ref, eye_ref, o_ref, lse_ref,
                           qs, ks, vs, os_, ls_, s_scr, p_scr, *, n_tiles, n_cls, c, cls_len):
    for t in range(n_tiles):
        for cl in range(n_cls):
            rows = slice(cl * cls_len + t * c, cl * cls_len + (t + 1) * c)
            qs[rows, :] = q_ref[t, cl]
            ks[rows, :] = k_ref[t, cl]
            vs[rows, :] = v_ref[t, cl]
    cst = _band_consts()
    nq = cls_len // A_QBLK

    def body(blk, carry):
        base = (blk // nq) * cls_len
        q0 = (blk % nq) * A_QBLK
        k0 = _window_start(q0, cls_len)
        _band_block(cst, qs, ks, vs, bias_ref, eye_ref, os_, ls_, s_scr, p_scr,
                    base + q0, base + k0, base + q0, (q0 - k0) // A_HALF)
        return carry

    lax.fori_loop(0, n_cls * nq, body, 0, unroll=2)
    for t in range(n_tiles):
        for cl in range(n_cls):
            rows = slice(cl * cls_len + t * c, cl * cls_len + (t + 1) * c)
            o_ref[t, cl] = os_[rows, :]
            lse_ref[t, cl] = ls_[rows, :]


def _attn_a(qkv_a, band_bias, eye2, b, s, gi, dil):
    cls_len = s // dil
    assert cls_len >= A_KWIN and cls_len % A_QBLK == 0 and (cls_len // A_QBLK) % 2 == 0
    stage = [pltpu.VMEM((A_HEADS // 2, 2 * A_QBLK, A_KWIN), F32),
             pltpu.VMEM((A_HEADS // 2, 2 * A_QBLK, A_KWIN), BF16)]
    const_specs = [pl.BlockSpec(band_bias.shape, lambda *_: (0, 0, 0)),
                   pl.BlockSpec(eye2.shape, lambda *_: (0, 0))]
    if dil == 1:
        tq = min(cls_len, 1024)
        view = qkv_a.reshape(b, s, A_QKV)
        kern = functools.partial(_attn_a_natural_kernel, tq=tq, cls_len=cls_len)
        o, lse = pl.pallas_call(
            kern,
            grid=(b, cls_len // tq),
            in_specs=[
                pl.BlockSpec((None, tq, BRANCH_WIDTH), lambda bi, t: (bi, t, 3 * gi)),
                pl.BlockSpec((None, cls_len, BRANCH_WIDTH), lambda bi, t: (bi, 0, 3 * gi + 1)),
                pl.BlockSpec((None, cls_len, BRANCH_WIDTH), lambda bi, t: (bi, 0, 3 * gi + 2)),
            ] + const_specs,
            out_specs=[
                pl.BlockSpec((None, tq, BRANCH_WIDTH), lambda bi, t: (bi, t, 0)),
                pl.BlockSpec((None, tq, LANES), lambda bi, t: (bi, t, 0)),
            ],
            out_shape=[jax.ShapeDtypeStruct((b, s, BRANCH_WIDTH), BF16),
                       jax.ShapeDtypeStruct((b, s, LANES), F32)],
            scratch_shapes=stage,
            compiler_params=_cparams(("parallel", "arbitrary")),
            name=f"attn_a{gi}",
        )(view, view, view, band_bias, eye2)
    else:
        n_tiles = s // PERM_TILE
        c = PERM_TILE // dil
        n_cls = max(1, 1024 // cls_len)
        assert dil % n_cls == 0 and c % 16 == 0
        view = qkv_a.reshape(b, n_tiles, dil, c, A_QKV)
        kern = functools.partial(_attn_a_classes_kernel, n_tiles=n_tiles, n_cls=n_cls, c=c, cls_len=cls_len)
        rows = n_cls * cls_len
        blk = lambda width: (None, n_tiles, n_cls, c, width)
        o, lse = pl.pallas_call(
            kern,
            grid=(b, dil // n_cls),
            in_specs=[
                pl.BlockSpec(blk(BRANCH_WIDTH), lambda bi, r: (bi, 0, r, 0, 3 * gi)),
                pl.BlockSpec(blk(BRANCH_WIDTH), lambda bi, r: (bi, 0, r, 0, 3 * gi + 1)),
                pl.BlockSpec(blk(BRANCH_WIDTH), lambda bi, r: (bi, 0, r, 0, 3 * gi + 2)),
            ] + const_specs,
            out_specs=[
                pl.BlockSpec(blk(BRANCH_WIDTH), lambda bi, r: (bi, 0, r, 0, 0)),
                pl.BlockSpec(blk(LANES), lambda bi, r: (bi, 0, r, 0, 0)),
            ],
            out_shape=[jax.ShapeDtypeStruct((b, n_tiles, dil, c, BRANCH_WIDTH), BF16),
                       jax.ShapeDtypeStruct((b, n_tiles, dil, c, LANES), F32)],
            scratch_shapes=[pltpu.VMEM((rows, BRANCH_WIDTH), BF16)] * 4
                           + [pltpu.VMEM((rows, LANES), F32)] + stage,
            compiler_params=_cparams(("parallel", "arbitrary")),
            name=f"attn_a{gi}",
        )(view, view, view, band_bias, eye2)
    return o.reshape(b * s, BRANCH_WIDTH), lse.reshape(b * s, LANES)


def _attn_b_kernel(q_ref, k_ref, v_ref, o_ref, kdup, vext, *, tq):
    s = k_ref.shape[0]

    @pl.when(pl.program_id(1) == 0)
    def _():
        k = k_ref[...]
        v = v_ref[...]
        ones = jnp.ones((s, LANES), BF16)
        for kh in range(B_KV_HEADS):
            ks = slice(kh * HEAD_DIM, (kh + 1) * HEAD_DIM)
            kdup[kh] = jnp.concatenate([k[:, ks], k[:, ks]], axis=1)
            vext[kh] = jnp.concatenate([v[:, ks], v[:, ks], ones], axis=1)

    sub = B_ROWS
    first_half = lax.broadcasted_iota(jnp.int32, (sub, LANES), 1) < HEAD_DIM
    pairs_per_kv = B_HEADS // B_KV_HEADS // 2
    nt = (((1,), (1,)), ((), ()))
    for hh in range(tq // sub):
        rows = slice(hh * sub, (hh + 1) * sub)
        for j in range(B_HEADS // 2):
            kh = j // pairs_per_kv
            ps = slice(j * LANES, (j + 1) * LANES)
            qp = q_ref[rows, ps]
            zero = jnp.zeros_like(qp)
            q2 = jnp.concatenate([jnp.where(first_half, qp, zero), jnp.where(first_half, zero, qp)], axis=0)
            sc = lax.dot_general(q2, kdup[kh], nt, preferred_element_type=F32)
            mx = jnp.max(sc, axis=-1, keepdims=True)
            p = jnp.exp(sc - mx).astype(BF16)
            ov = jnp.dot(p, vext[kh], preferred_element_type=F32)
            o_sel = jnp.where(first_half, ov[:sub, :LANES], ov[sub:, :LANES])
            l_sel = jnp.where(first_half, ov[:sub, LANES:], ov[sub:, LANES:])
            o_ref[rows, ps] = (o_sel / l_sel).astype(BF16)


def _attn_b(qkv_b, b, s, tq):
    view = qkv_b.reshape(b, s, B_QKV)
    o = pl.pallas_call(
        functools.partial(_attn_b_kernel, tq=tq),
        grid=(b, s // tq),
        in_specs=[
            pl.BlockSpec((None, tq, B_Q), lambda bi, t: (bi, t, 0)),
            pl.BlockSpec((None, s, B_KV), lambda bi, t: (bi, 0, B_Q // B_KV)),
            pl.BlockSpec((None, s, B_KV), lambda bi, t: (bi, 0, B_Q // B_KV + 1)),
        ],
        out_specs=pl.BlockSpec((None, tq, B_Q), lambda bi, t: (bi, t, 0)),
        out_shape=jax.ShapeDtypeStruct((b, s, B_Q), BF16),
        scratch_shapes=[pltpu.VMEM((B_KV_HEADS, s, LANES), BF16),
                        pltpu.VMEM((B_KV_HEADS, s, 2 * LANES), BF16)],
        compiler_params=_cparams(("parallel", "arbitrary")),
        name="attn_b",
    )(view, view, view)
    return o.reshape(b * s, B_Q)


def _attn_m_kernel(q_ref, k_ref, v_ref, o_ref):
    scale = M_HEAD_DIM ** -0.5
    outs = []
    for h in range(M_HEADS):
        hs = slice(h * M_HEAD_DIM, (h + 1) * M_HEAD_DIM)
        sc = lax.dot_general(q_ref[:, hs], k_ref[:, hs], (((1,), (1,)), ((), ())),
                             preferred_element_type=F32) * scale
        mx = jnp.max(sc, axis=-1, keepdims=True)
        p = jnp.exp(sc - mx)
        l = jnp.sum(p, axis=-1, keepdims=True)
        o = jnp.dot(p.astype(BF16), v_ref[:, hs], preferred_element_type=F32)
        outs.append(o / l)
    o_ref[...] = jnp.concatenate(outs, axis=1).astype(BF16)


def _attn_m(mq, kvm, b, s, n_mem, tq):
    qv = mq.reshape(b, s, M_Q)
    kv = kvm.reshape(b, n_mem, 2 * M_Q)
    o = pl.pallas_call(
        _attn_m_kernel,
        grid=(b, s // tq),
        in_specs=[
            pl.BlockSpec((None, tq, M_Q), lambda bi, t: (bi, t, 0)),
            pl.BlockSpec((None, n_mem, M_Q), lambda bi, t: (bi, 0, 0)),
            pl.BlockSpec((None, n_mem, M_Q), lambda bi, t: (bi, 0, 1)),
        ],
        out_specs=pl.BlockSpec((None, tq, M_Q), lambda bi, t: (bi, t, 0)),
        out_shape=jax.ShapeDtypeStruct((b, s, M_Q), BF16),
        compiler_params=_cparams(("parallel", "arbitrary")),
        name="attn_m",
    )(qv, kv, kv)
    return o.reshape(b * s, M_Q)


def _final_kernel(x_ref, oa0_ref, oa1_ref, oa2_ref, l0_ref, l1_ref, l2_ref, ob_ref, om_ref,
                  gpre_ref, wg_ref, bm_ref, wa_ref, wb_ref, wm_ref, wout_ref, gpost_ref,
                  exp_ref, o_ref, nat_o, nat_l, *, tm, dils):
    n_ch = BRANCH_WIDTH // LANES
    group_refs = list(zip(dils, (oa0_ref, oa1_ref, oa2_ref), (l0_ref, l1_ref, l2_ref)))

    for gi, (dil, og_ref, l_ref) in enumerate(group_refs):
        if dil == 1:
            continue
        cc = tm // dil
        for r in range(dil):
            o = og_ref[r].astype(F32)
            for ch in range(n_ch):
                nat_o[gi * n_ch + ch, pl.ds(r, cc, stride=dil), :] = o[:, ch * LANES:(ch + 1) * LANES]
            nat_l[gi, pl.ds(r, cc, stride=dil), :] = l_ref[r]

    expand = exp_ref[...]
    for sb in range(tm // FINAL_ROWS):
        rows = slice(sb * FINAL_ROWS, (sb + 1) * FINAL_ROWS)
        xf = x_ref[rows, :]
        h = _rms_norm_rows(xf, gpre_ref[...]).astype(BF16)
        og, lses = [], []
        for gi, (dil, og_ref, l_ref) in enumerate(group_refs):
            if dil == 1:
                o = og_ref[rows, :].astype(F32)
                og.append([o[:, ch * LANES:(ch + 1) * LANES] for ch in range(n_ch)])
                lses.append(l_ref[rows, :])
            else:
                og.append([nat_o[gi * n_ch + ch, rows, :] for ch in range(n_ch)])
                lses.append(nat_l[gi, rows, :])

        mx = functools.reduce(jnp.maximum, lses)
        es = [jnp.exp(l - mx) for l in lses]
        inv_z = 1.0 / functools.reduce(lambda a, c: a + c, es)
        oa_ch = [None] * n_ch
        rest = None
        for gi, (e, o_chunks) in enumerate(zip(es, og)):
            if gi + 1 < len(es):
                w = e * inv_z
                hi = w.astype(BF16)
                lo = (w - hi.astype(F32)).astype(BF16)
                w_wide = (jnp.dot(hi, expand, preferred_element_type=F32)
                          + jnp.dot(lo, expand, preferred_element_type=F32))
                rest = 1.0 - w_wide if rest is None else rest - w_wide
            else:
                w_wide = rest
            for ch in range(n_ch):
                term = w_wide[:, ch * LANES:(ch + 1) * LANES] * o_chunks[ch]
                oa_ch[ch] = term if oa_ch[ch] is None else oa_ch[ch] + term
        oa = jnp.concatenate(oa_ch, axis=1)

        def branch(o, idx, w_ref):
            gs = slice(idx * BRANCH_WIDTH, (idx + 1) * BRANCH_WIDTH)
            ms = slice(3 * BRANCH_WIDTH + idx * D_MODEL, 3 * BRANCH_WIDTH + (idx + 1) * D_MODEL)
            ms_b = slice(idx * D_MODEL, (idx + 1) * D_MODEL)
            gate = jnp.dot(h, wg_ref[:, gs], preferred_element_type=F32)
            gate = gate * jax.nn.sigmoid(gate)
            y = jnp.dot((o * gate).astype(BF16), w_ref[...], preferred_element_type=F32)
            mg = jnp.dot(h, wg_ref[:, ms], preferred_element_type=F32) + bm_ref[:, ms_b]
            return jax.nn.sigmoid(mg) * y

        merged = branch(oa, 0, wa_ref)
        merged = merged + branch(ob_ref[rows, :].astype(F32), 1, wb_ref)
        merged = merged + branch(om_ref[rows, :].astype(F32), 2, wm_ref)
        out = jnp.dot(merged.astype(BF16), wout_ref[...], preferred_element_type=F32)
        o_ref[rows, :] = xf + _rms_norm_rows(out, gpost_ref[...])


def _final(x2, oas, lses, ob, om, g_pre, w_g, b_merge, w_a, w_b, w_m, w_out, g_post, expand, dils, tm):
    m = x2.shape[0]
    per_tile = PERM_TILE // tm
    row = lambda n: pl.BlockSpec((tm, n), lambda i: (i, 0))
    full = lambda a: pl.BlockSpec(a.shape, lambda i: (0, 0), pipeline_mode=pl.Buffered(1))

    def group_operand(a, dil, width):
        if dil == 1:
            return a, row(width)
        cc = tm // dil
        view = a.reshape(m // PERM_TILE, dil, per_tile, cc, width)
        return view, pl.BlockSpec((None, dil, None, cc, width),
                                  lambda i: (i // per_tile, 0, i % per_tile, 0, 0))

    o_ops = [group_operand(a, dil, BRANCH_WIDTH) for a, dil in zip(oas, dils)]
    l_ops = [group_operand(a, dil, LANES) for a, dil in zip(lses, dils)]
    consts = (g_pre, w_g, b_merge, w_a, w_b, w_m, w_out, g_post, expand)
    kern = functools.partial(_final_kernel, tm=tm, dils=dils)
    n_ch = BRANCH_WIDTH // LANES
    return pl.pallas_call(
        kern,
        grid=(m // tm,),
        in_specs=([row(D_MODEL)] + [spec for _, spec in o_ops] + [spec for _, spec in l_ops]
                  + [row(BRANCH_WIDTH)] * 2 + [full(a) for a in consts]),
        out_specs=row(D_MODEL),
        out_shape=jax.ShapeDtypeStruct((m, D_MODEL), F32),
        scratch_shapes=[pltpu.VMEM((len(dils) * n_ch, tm, LANES), F32),
                        pltpu.VMEM((len(dils), tm, LANES), F32)],
        compiler_params=_cparams(("parallel",)),
        name="final",
    )(x2, *[a for a, _ in o_ops], *[a for a, _ in l_ops], ob, om, *consts)


def _rotary_tables(pos_per_lane, inv_per_lane, is_second_half):
    ang = pos_per_lane.astype(np.float64) * inv_per_lane[None, :]
    cos, sin = np.cos(ang), np.sin(ang)
    zero = np.zeros_like(sin)
    return np.stack([cos,
                     np.where(is_second_half[None, :], sin, zero),
                     np.where(is_second_half[None, :], zero, -sin)]).astype(np.float32)


def _class_major_positions(s, dil):
    n = np.arange(s)
    within = n % PERM_TILE
    c = PERM_TILE // dil
    return (n // PERM_TILE) * PERM_TILE + (within % c) * dil + within // c


def _tables_a(s, dils):
    half = HEAD_DIM // 2
    lane = np.arange(LANES)
    inv = np.power(ROPE_THETA, -(lane % half).astype(np.float64) * 2.0 / HEAD_DIM)
    second = (lane % HEAD_DIM) >= half
    tables = []
    for dil in dils:
        pos = _class_major_positions(s, dil)
        tables.append(_rotary_tables(np.broadcast_to(pos[:, None], (s, LANES)), inv, second))
    return jnp.asarray(np.stack(tables))


def _band_tables():
    q = np.arange(A_QBLK)[None, :, None]
    k = np.arange(A_KWIN)[None, None, :]
    delta = (np.arange(3) * A_HALF)[:, None, None]
    bias = np.where(np.abs(q - k + delta) <= A_HALF, 0.0, NEG_INF).astype(np.float32)
    eye2 = np.concatenate([np.eye(A_QBLK), np.eye(A_QBLK)], axis=0)
    return jnp.asarray(bias.transpose(0, 2, 1), BF16), jnp.asarray(eye2, BF16)


def _tables_b(s):
    dr = HEAD_DIM // 2
    half = dr // 2
    lane = np.arange(LANES)
    inv = np.power(ROPE_THETA, -(lane % half).astype(np.float64) * 2.0 / dr)
    row_pos = np.repeat(np.arange(s // GRID_W), GRID_W)
    col_pos = np.arange(s) % GRID_W
    uses_col = (lane % HEAD_DIM) >= dr
    pos = np.where(uses_col[None, :], col_pos[:, None], row_pos[:, None])
    return jnp.asarray(_rotary_tables(pos, inv, (lane % dr) >= half))


def kernel(x, mem, g_pre, w_in, b_merge, q_norm, k_norm, g_mem, w_mem_kv, w_br_a, w_br_b, w_br_m, w_out, g_post):
    b, s, d = x.shape
    n_mem = mem.shape[1]
    depth = g_pre.shape[0]
    n_groups = len(A_GROUPS)

    dils = tuple(dil for _, dil in A_GROUPS)
    assert s % PERM_TILE == 0
    t_a = _tables_a(s, dils)
    band_bias, eye2 = _band_tables()
    t_b = _tables_b(s)
    lane = np.arange(LANES)
    ones_bd = jnp.asarray((lane[:, None] // HEAD_DIM) == (lane[None, :] // HEAD_DIM), BF16)
    wide = np.arange(BRANCH_WIDTH)
    expand = jnp.asarray(lane[:, None] == (wide[None, :] // HEAD_DIM) * LSE_LANES_PER_HEAD, BF16)

    for layer in range(depth):
        w = w_in[layer]
        w_a = w[:, :A_QKV].reshape(d, 3, n_groups, BRANCH_WIDTH)
        w_a = w_a * jnp.asarray([HEAD_DIM ** -0.5, 1.0, 1.0], F32)[None, :, None, None]
        w_a = w_a.transpose(0, 2, 1, 3).reshape(d, A_QKV).astype(BF16)
        off = A_QKV
        w_bm = w[:, off:off + B_QKV + M_Q].astype(BF16)
        off += B_QKV + M_Q
        w_g = w[:, off:].astype(BF16)
        gq = jnp.tile(q_norm[layer] * HEAD_DIM ** -0.5, LANES // HEAD_DIM)[None, :]
        gk = jnp.tile(k_norm[layer], LANES // HEAD_DIM)[None, :]

        x2 = x.reshape(b * s, d)
        gp = g_pre[layer][None, :]
        qkv_a = _proj_a(x2, gp, w_a, t_a, s, dils)
        qkv_b, mq = _proj_bm(x2, gp, w_bm, t_b, gq, gk, ones_bd, s, tm=1024)
        kvm = _mem_kv(mem.reshape(b * n_mem, d), g_mem[layer][None, :],
                      w_mem_kv[layer].astype(BF16), tm=n_mem)

        oas, lses = [], []
        for gi, (_, dil) in enumerate(A_GROUPS):
            o_g, lse_g = _attn_a(qkv_a, band_bias, eye2, b, s, gi, dil)
            oas.append(o_g)
            lses.append(lse_g)
        ob = _attn_b(qkv_b, b, s, tq=512)
        om = _attn_m(mq, kvm, b, s, n_mem, tq=1024)

        x2 = _final(x2, oas, lses, ob, om, gp, w_g, b_merge[layer][None, :],
                    w_br_a[layer].astype(BF16), w_br_b[layer].astype(BF16),
                    w_br_m[layer].astype(BF16), w_out[layer].astype(BF16),
                    g_post[layer][None, :], expand, dils, tm=512)
        x = x2.reshape(b, s, d)
    return x
```
